```python
import jax, jax.numpy as jnp
from jax import lax
import numpy as np

D_MODEL = 2048
BATCH = 4
SEQ = 8192
DEPTH = 1

GRID_W = 64
Q_BLOCK = 128
ROPE_THETA = 10000.0
EPS = 1e-6

A_HEADS = 8
A_KV_HEADS = 2
A_HEAD_DIM = 128
A_GROUP = A_HEADS // A_KV_HEADS
A_WIDTH = A_HEADS * A_HEAD_DIM
B_HEADS = 8
B_QK_DIM = 64
B_V_DIM = 2 * B_QK_DIM
B_WIDTH = B_HEADS * B_V_DIM
A_Q_COLS = A_HEADS * A_HEAD_DIM
A_KV_COLS = A_KV_HEADS * A_HEAD_DIM
B_QK_COLS = B_HEADS * 2 * B_QK_DIM
B_V_COLS = B_HEADS * B_V_DIM
IN_COLS = A_Q_COLS + 2 * A_KV_COLS + 2 * B_QK_COLS + B_V_COLS
N_BRANCHES = 2
N_GROUPS = 4
EXPERTS_PER_GROUP = 8
N_EXPERTS = N_GROUPS * EXPERTS_PER_GROUP
TOP_K = 2
EXPERT_FF = D_MODEL // 4
MOE_BLOCK = 128

kernel_name = "hybrid_gated_axialgqa_diffattn_hmoe"


def rmsnorm(x, g):
    xf = x.astype(jnp.float32)
    xf = xf * lax.rsqrt(jnp.mean(xf * xf, axis=-1, keepdims=True) + EPS)
    return xf.astype(x.dtype) * g


def rope_angles(pos, dim):
    inv = ROPE_THETA ** (-jnp.arange(0, dim, 2, dtype=jnp.float32) / dim)
    ang = pos.astype(jnp.float32)[:, None] * inv[None, :]
    return jnp.cos(ang), jnp.sin(ang)


def apply_rope(x, cos, sin):
    half = x.shape[-1] // 2
    xf = x.astype(jnp.float32)
    x1, x2 = xf[..., :half], xf[..., half:]
    out = jnp.concatenate([x1 * cos - x2 * sin, x2 * cos + x1 * sin], axis=-1)
    return out.astype(x.dtype)


def axial_gqa(q, k, v, q_g, k_g, rows, cols):
    q = rmsnorm(q, q_g)
    k = rmsnorm(k, k_g)
    half = A_HEAD_DIM // 2
    cr, sr = rope_angles(rows, half)
    cc, sc = rope_angles(cols, half)
    bc = lambda t: t[None, :, None, :]

    def axial(t):
        return jnp.concatenate([apply_rope(t[..., :half], bc(cr), bc(sr)),
                                apply_rope(t[..., half:], bc(cc), bc(sc))], axis=-1)

    q, k = axial(q), axial(k)
    b, s = q.shape[0], q.shape[1]
    nb = s // Q_BLOCK
    qb = q.reshape(b, nb, Q_BLOCK, A_KV_HEADS, A_GROUP, A_HEAD_DIM).transpose(1, 0, 2, 3, 4, 5)
    scale = A_HEAD_DIM ** -0.5

    def block(qi):
        sc_ = jnp.einsum('bqkgd,bskd->bkgqs', qi, k, preferred_element_type=jnp.float32) * scale
        p = jax.nn.softmax(sc_, axis=-1).astype(v.dtype)
        return jnp.einsum('bkgqs,bskd->bqkgd', p, v)

    o = lax.map(block, qb)
    return o.transpose(1, 0, 2, 3, 4, 5).reshape(b, s, A_WIDTH)


def diff_attention(q, k, v, lam, lam_init, subln_g, pos):
    c, s_ = rope_angles(pos, B_QK_DIM)
    bc = lambda t: t[None, :, None, None, :]
    q = apply_rope(q, bc(c), bc(s_))
    k = apply_rope(k, bc(c), bc(s_))
    b, s = q.shape[0], q.shape[1]
    nb = s // Q_BLOCK
    qb = q.reshape(b, nb, Q_BLOCK, B_HEADS, 2, B_QK_DIM).transpose(1, 0, 2, 3, 4, 5)
    scale = B_QK_DIM ** -0.5

    def block(qi):
        sc_ = jnp.einsum('bqhcd,bshcd->bhcqs', qi, k, preferred_element_type=jnp.float32) * scale
        p = jax.nn.softmax(sc_, axis=-1)
        a = (p[:, :, 0] - lam * p[:, :, 1]).astype(v.dtype)
        return jnp.einsum('bhqs,bshe->bqhe', a, v)

    o = lax.map(block, qb)
    o = o.transpose(1, 0, 2, 3, 4).reshape(b, s, B_HEADS, B_V_DIM)
    o = rmsnorm(o, subln_g) * (1.0 - lam_init)
    return o.reshape(b, s, B_WIDTH)


def hier_moe(h, w_rg, w_re, w1, w3, w2):
    b, s, d = h.shape
    t = b * s
    hf = h.reshape(t, d)
    g_logits = jnp.matmul(hf, w_rg).astype(jnp.float32)
    g_prob = jax.nn.softmax(g_logits, axis=-1)
    g_idx = jnp.argmax(g_logits, axis=-1).astype(jnp.int32)
    g_w = jnp.take_along_axis(g_prob, g_idx[:, None], axis=-1)
    e_logits = jnp.matmul(hf, w_re).astype(jnp.float32).reshape(t, N_GROUPS, EXPERTS_PER_GROUP)
    e_in_group = jnp.take_along_axis(e_logits, g_idx[:, None, None], axis=1)[:, 0]
    top_v, top_i = lax.top_k(e_in_group, TOP_K)
    e_w = jax.nn.softmax(top_v, axis=-1) * g_w
    expert = g_idx[:, None] * EXPERTS_PER_GROUP + top_i.astype(jnp.int32)

    n_assign = t * TOP_K
    e_flat = expert.reshape(n_assign)
    tok_flat = jnp.repeat(jnp.arange(t, dtype=jnp.int32), TOP_K)
    w_flat = e_w.reshape(n_assign)
    order = jnp.argsort(e_flat)
    e_sorted, tok_sorted, w_sorted = e_flat[order], tok_flat[order], w_flat[order]
    counts = jnp.bincount(e_flat, length=N_EXPERTS).astype(jnp.int32)
    offsets = jnp.cumsum(counts) - counts
    padded = ((counts + MOE_BLOCK - 1) // MOE_BLOCK) * MOE_BLOCK
    padded_ends = jnp.cumsum(padded)
    padded_offsets = padded_ends - padded
    rank = jnp.arange(n_assign, dtype=jnp.int32) - offsets[e_sorted]
    pos = padded_offsets[e_sorted] + rank
    n_blocks = -(-n_assign // MOE_BLOCK) + N_EXPERTS
    p_len = n_blocks * MOE_BLOCK
    buf_tok = jnp.full((p_len,), t, jnp.int32).at[pos].set(tok_sorted)
    buf_w = jnp.zeros((p_len,), jnp.float32).at[pos].set(w_sorted)
    block_start = jnp.arange(n_blocks, dtype=jnp.int32) * MOE_BLOCK
    block_expert = jnp.clip(jnp.searchsorted(padded_ends, block_start, side='right'),
                            0, N_EXPERTS - 1).astype(jnp.int32)
    x_pad = jnp.concatenate([hf, jnp.zeros((1, d), hf.dtype)], axis=0)

    def run(args):
        tok, wt, e = args
        xb = x_pad[tok]
        u = jax.nn.silu(xb @ w1[e]) * (xb @ w3[e])
        return (u @ w2[e]) * wt[:, None].astype(hf.dtype)

    out = lax.map(run, (buf_tok.reshape(n_blocks, MOE_BLOCK),
                        buf_w.reshape(n_blocks, MOE_BLOCK), block_expert))
    y = jnp.zeros((t + 1, d), hf.dtype).at[buf_tok].add(out.reshape(p_len, d))
    return y[:t].reshape(b, s, d)


def setup_inputs(seed: int = 0) -> dict:
    key = jax.random.key(seed)
    ks = jax.random.split(key, 24)
    f32 = jnp.float32
    nrm = lambda k, shape, scale: jax.random.normal(k, shape, f32) * scale
    gain = lambda k, shape: 1.0 + 0.01 * jax.random.normal(k, shape, f32)
    L, D = DEPTH, D_MODEL
    return {
        "x": jax.random.normal(ks[0], (BATCH, SEQ, D), f32),
        "g_mix": gain(ks[1], (L, D)),
        "w_in": nrm(ks[2], (L, D, IN_COLS), D ** -0.5),
        "q_norm_a": gain(ks[3], (L, A_HEAD_DIM)),
        "k_norm_a": gain(ks[4], (L, A_HEAD_DIM)),
        "lam_q1": nrm(ks[5], (L, B_QK_DIM), 0.1),
        "lam_k1": nrm(ks[6], (L, B_QK_DIM), 0.1),
        "lam_q2": nrm(ks[7], (L, B_QK_DIM), 0.1),
        "lam_k2": nrm(ks[8], (L, B_QK_DIM), 0.1),
        "subln_b": gain(ks[9], (L, B_V_DIM)),
        "w_branch_a": nrm(ks[10], (L, A_WIDTH, D), A_WIDTH ** -0.5),
        "w_branch_b": nrm(ks[11], (L, B_WIDTH, D), B_WIDTH ** -0.5),
        "w_gate": nrm(ks[12], (L, D, N_BRANCHES * D), D ** -0.5),
        "b_gate": nrm(ks[13], (L, N_BRANCHES * D), 0.01),
        "w_out": nrm(ks[14], (L, D, D), D ** -0.5),
        "g_ffn": gain(ks[15], (L, D)),
        "w_router_group": nrm(ks[16], (L, D, N_GROUPS), D ** -0.5),
        "w_router_expert": nrm(ks[17], (L, D, N_EXPERTS), D ** -0.5),
        "w_e_gate": nrm(ks[18], (L, N_EXPERTS, D, EXPERT_FF), D ** -0.5),
        "w_e_up": nrm(ks[19], (L, N_EXPERTS, D, EXPERT_FF), D ** -0.5),
        "w_e_down": nrm(ks[20], (L, N_EXPERTS, EXPERT_FF, D), EXPERT_FF ** -0.5),
        "g_final": gain(ks[21], (D,)),
    }


def reference(x, g_mix, w_in, q_norm_a, k_norm_a, lam_q1, lam_k1, lam_q2, lam_k2, subln_b,
              w_branch_a, w_branch_b, w_gate, b_gate, w_out, g_ffn, w_router_group,
              w_router_expert, w_e_gate, w_e_up, w_e_down, g_final):
    b, s, _ = x.shape
    ROWS = s // GRID_W
    pos = jnp.arange(s, dtype=jnp.int32)
    rows = jnp.repeat(jnp.arange(ROWS, dtype=jnp.int32), GRID_W)
    cols = jnp.tile(jnp.arange(GRID_W, dtype=jnp.int32), ROWS)
    split_at = [A_Q_COLS, A_Q_COLS + A_KV_COLS, A_Q_COLS + 2 * A_KV_COLS,
                A_Q_COLS + 2 * A_KV_COLS + B_QK_COLS, A_Q_COLS + 2 * A_KV_COLS + 2 * B_QK_COLS]
    for l in range(DEPTH):
        lam_init = 0.8 - 0.6 * float(np.exp(-0.3 * l))
        h = rmsnorm(x, g_mix[l])
        proj = h @ w_in[l]
        qa, ka, va, qb, kb, vb = jnp.split(proj, split_at, axis=-1)
        qa = qa.reshape(b, s, A_HEADS, A_HEAD_DIM)
        ka = ka.reshape(b, s, A_KV_HEADS, A_HEAD_DIM)
        va = va.reshape(b, s, A_KV_HEADS, A_HEAD_DIM)
        qb = qb.reshape(b, s, B_HEADS, 2, B_QK_DIM)
        kb = kb.reshape(b, s, B_HEADS, 2, B_QK_DIM)
        vb = vb.reshape(b, s, B_HEADS, B_V_DIM)
        o_a = axial_gqa(qa, ka, va, q_norm_a[l], k_norm_a[l], rows, cols)
        lam = (jnp.exp(jnp.sum(lam_q1[l].astype(jnp.float32) * lam_k1[l].astype(jnp.float32)))
               - jnp.exp(jnp.sum(lam_q2[l].astype(jnp.float32) * lam_k2[l].astype(jnp.float32)))
               + lam_init)
        o_b = diff_attention(qb, kb, vb, lam, lam_init, subln_b[l], pos)
        gates = jax.nn.sigmoid((h @ w_gate[l] + b_gate[l]).astype(jnp.float32)).astype(x.dtype)
        g_a, g_b = jnp.split(gates, N_BRANCHES, axis=-1)
        merged = g_a * (o_a @ w_branch_a[l]) + g_b * (o_b @ w_branch_b[l])
        x = x + merged @ w_out[l]
        x = x + hier_moe(rmsnorm(x, g_ffn[l]), w_router_group[l], w_router_expert[l],
                         w_e_gate[l], w_e_up[l], w_e_down[l])
    return rmsnorm(x, g_final)
```

```python
import functools
import math

import jax
import jax.numpy as jnp
from jax import lax
from jax.experimental import pallas as pl
from jax.experimental.pallas import tpu as pltpu

F32 = jnp.float32
BF16 = jnp.bfloat16

GRID_W = 64
ROPE_THETA = 10000.0
EPS = 1e-6
HEAD = 128
ROPE_HALF = 32
A_HEADS = 8
A_KV_HEADS = 2
A_GROUP = A_HEADS // A_KV_HEADS
B_HEADS = 8
B_QK_DIM = 64
N_GROUPS = 4
EXPERTS_PER_GROUP = 8
N_EXPERTS = N_GROUPS * EXPERTS_PER_GROUP
TOP_K = 2
ROUTER_LANES = 128

VMEM_LIMIT = 56 * 1024 * 1024

ROW_TILE = 512
MERGE_TILE = 256
Q_TILE = 256
KV_CHUNK = 512
MOE_BLOCK = 256


def _params(sem):
    return pltpu.CompilerParams(dimension_semantics=sem, vmem_limit_bytes=VMEM_LIMIT)


def _rmsnorm_kernel(x_ref, g_ref, o_ref):
    x = x_ref[...]
    ms = jnp.mean(x * x, axis=-1, keepdims=True)
    o_ref[...] = (x * lax.rsqrt(ms + EPS) * g_ref[...]).astype(o_ref.dtype)


def _rmsnorm(x, g, out_dtype):
    t, d = x.shape
    return pl.pallas_call(
        _rmsnorm_kernel,
        out_shape=jax.ShapeDtypeStruct((t, d), out_dtype),
        grid=(t // ROW_TILE,),
        in_specs=[pl.BlockSpec((ROW_TILE, d), lambda i: (i, 0)),
                  pl.BlockSpec((1, d), lambda i: (0, 0))],
        out_specs=pl.BlockSpec((ROW_TILE, d), lambda i: (i, 0)),
        compiler_params=_params(("parallel",)),
        name="rmsnorm",
    )(x, g.reshape(1, d))


def _rotary(t, cos, sin):
    lane = lax.broadcasted_iota(jnp.int32, t.shape, 1)
    first = (lane % (2 * ROPE_HALF)) < ROPE_HALF
    partner = jnp.where(first, pltpu.roll(t, HEAD - ROPE_HALF, 1), pltpu.roll(t, ROPE_HALF, 1))
    return t * cos + partner * sin


def _proj_rope_kernel(h_ref, w_ref, cos_ref, sin_ref, g_ref, o_ref, *, normalize):
    acc = jnp.dot(h_ref[...], w_ref[...], preferred_element_type=F32)
    cos = cos_ref[...]
    sin = sin_ref[...]
    for c in range(acc.shape[1] // HEAD):
        t = acc[:, c * HEAD:(c + 1) * HEAD]
        if normalize:
            ms = jnp.mean(t * t, axis=-1, keepdims=True)
            t = t * lax.rsqrt(ms + EPS) * g_ref[...]
        o_ref[:, c * HEAD:(c + 1) * HEAD] = _rotary(t, cos, sin).astype(o_ref.dtype)


def _proj_rope(h, w, cos, sin, gain, *, normalize, seq):
    t, d = h.shape
    n = w.shape[1]
    tiles_per_seq = seq // ROW_TILE
    return pl.pallas_call(
        functools.partial(_proj_rope_kernel, normalize=normalize),
        out_shape=jax.ShapeDtypeStruct((t, n), BF16),
        grid=(t // ROW_TILE,),
        in_specs=[pl.BlockSpec((ROW_TILE, d), lambda i: (i, 0)),
                  pl.BlockSpec((d, n), lambda i: (0, 0)),
                  pl.BlockSpec((ROW_TILE, HEAD), lambda i: (i % tiles_per_seq, 0)),
                  pl.BlockSpec((ROW_TILE, HEAD), lambda i: (i % tiles_per_seq, 0)),
                  pl.BlockSpec((1, HEAD), lambda i: (0, 0))],
        out_specs=pl.BlockSpec((ROW_TILE, n), lambda i: (i, 0)),
        compiler_params=_params(("parallel",)),
        name="proj_rope",
    )(h, w, cos, sin, gain.reshape(1, HEAD))


def _proj_t_kernel(h_ref, wt_ref, o_ref):
    o_ref[...] = lax.dot_general(wt_ref[...], h_ref[...], (((1,), (1,)), ((), ())),
                                 preferred_element_type=F32).astype(o_ref.dtype)


def _proj_t(h, wt):
    t, d = h.shape
    n = wt.shape[0]
    return pl.pallas_call(
        _proj_t_kernel,
        out_shape=jax.ShapeDtypeStruct((n, t), BF16),
        grid=(t // ROW_TILE,),
        in_specs=[pl.BlockSpec((ROW_TILE, d), lambda i: (i, 0)),
                  pl.BlockSpec((n, d), lambda i: (0, 0))],
        out_specs=pl.BlockSpec((n, ROW_TILE), lambda i: (0, i)),
        compiler_params=_params(("parallel",)),
        name="proj_t",
    )(h, wt)


def _proj_gate_kernel(h_ref, w_ref, b_ref, o_ref):
    z = jnp.dot(h_ref[...], w_ref[...], preferred_element_type=F32) + b_ref[...]
    o_ref[...] = (1.0 / (1.0 + jnp.exp(-z))).astype(o_ref.dtype)


def _proj_gate(h, w, b):
    t, d = h.shape
    n = w.shape[1]
    tn = 1024
    return pl.pallas_call(
        _proj_gate_kernel,
        out_shape=jax.ShapeDtypeStruct((t, n), BF16),
        grid=(n // tn, t // ROW_TILE),
        in_specs=[pl.BlockSpec((ROW_TILE, d), lambda j, i: (i, 0)),
                  pl.BlockSpec((d, tn), lambda j, i: (0, j)),
                  pl.BlockSpec((1, tn), lambda j, i: (0, j))],
        out_specs=pl.BlockSpec((ROW_TILE, tn), lambda j, i: (i, j)),
        compiler_params=_params(("parallel", "parallel")),
        name="proj_gate",
    )(h, w, b.reshape(1, n))


def _attn_kernel(*refs, differential, lam_init):
    if differential:
        q_ref, k_ref, vt_ref, lq1_ref, lk1_ref, lq2_ref, lk2_ref, sub_ref, o_ref = refs
        q = q_ref[...]
        lane = lax.broadcasted_iota(jnp.int32, q.shape, 1)
        zero = jnp.zeros_like(q)
        qs = [jnp.where(lane < B_QK_DIM, q, zero), jnp.where(lane >= B_QK_DIM, q, zero)]
    else:
        q_ref, k_ref, vt_ref, o_ref = refs
        qs = [q_ref[:, g * HEAD:(g + 1) * HEAD] for g in range(q_ref.shape[1] // HEAD)]
    tq = q_ref.shape[0]
    n_chunks = k_ref.shape[0] // KV_CHUNK

    outs = []
    for qg in qs:
        def body(j, carry, qg=qg):
            m, l, acc = carry
            off = pl.multiple_of(j * KV_CHUNK, KV_CHUNK)
            kc = k_ref[pl.ds(off, KV_CHUNK), :]
            vc = vt_ref[:, pl.ds(off, KV_CHUNK)]
            st = lax.dot_general(kc, qg, (((1,), (1,)), ((), ())), preferred_element_type=F32)
            m_new = jnp.maximum(m, jnp.max(st, axis=0, keepdims=True))
            alpha = jnp.exp(m - m_new)
            p = jnp.exp(st - m_new)
            l = alpha * l + jnp.sum(p, axis=0, keepdims=True)
            acc = alpha * acc + jnp.dot(vc, p.astype(BF16), preferred_element_type=F32)
            return m_new, l, acc

        init = (jnp.full((1, tq), -jnp.inf, F32), jnp.zeros((1, tq), F32), jnp.zeros((HEAD, tq), F32))
        _, l, acc = lax.fori_loop(0, n_chunks, body, init)
        outs.append(acc / l)

    if differential:
        lam = (jnp.exp(jnp.sum(lq1_ref[...] * lk1_ref[...], axis=1, keepdims=True))
               - jnp.exp(jnp.sum(lq2_ref[...] * lk2_ref[...], axis=1, keepdims=True)) + lam_init)
        ot = outs[0] - lam * outs[1]
        ms = jnp.mean(ot * ot, axis=0, keepdims=True)
        ot = ot * lax.rsqrt(ms + EPS) * sub_ref[...] * (1.0 - lam_init)
        o_ref[...] = ot.T.astype(o_ref.dtype)
    else:
        for g, ot in enumerate(outs):
            o_ref[:, g * HEAD:(g + 1) * HEAD] = ot.T.astype(o_ref.dtype)


def _attention(q, k, vt, extra, *, batch, seq, kv_heads, q_cols, differential, lam_init):
    t = q.shape[0]
    nq = seq // Q_TILE
    in_specs = [pl.BlockSpec((Q_TILE, q_cols), lambda b, h, i: (b * nq + i, h)),
                pl.BlockSpec((seq, HEAD), lambda b, h, i: (b, h)),
                pl.BlockSpec((HEAD, seq), lambda b, h, i: (h, b))]
    in_specs += [pl.BlockSpec(e.shape, lambda b, h, i: (0, 0)) for e in extra]
    return pl.pallas_call(
        functools.partial(_attn_kernel, differential=differential, lam_init=lam_init),
        out_shape=jax.ShapeDtypeStruct((t, kv_heads * q_cols), BF16),
        grid=(batch, kv_heads, nq),
        in_specs=in_specs,
        out_specs=pl.BlockSpec((Q_TILE, q_cols), lambda b, h, i: (b * nq + i, h)),
        compiler_params=_params(("parallel", "parallel", "parallel")),
        name="diff_attention" if differential else "axial_gqa",
    )(q, k, vt, *extra)


def _merge_kernel(oa_ref, ob_ref, ga_ref, gb_ref, x_ref, wba_ref, wbb_ref, wout_ref, gffn_ref, wr_ref,
                  x1_ref, h2_ref, re_ref, rw_ref):
    pa = jnp.dot(oa_ref[...], wba_ref[...], preferred_element_type=F32)
    pb = jnp.dot(ob_ref[...], wbb_ref[...], preferred_element_type=F32)
    merged = ga_ref[...].astype(F32) * pa + gb_ref[...].astype(F32) * pb
    x1 = x_ref[...] + jnp.dot(merged.astype(BF16), wout_ref[...], preferred_element_type=F32)
    x1_ref[...] = x1
    ms = jnp.mean(x1 * x1, axis=-1, keepdims=True)
    h2 = x1 * lax.rsqrt(ms + EPS) * gffn_ref[...]
    h2_ref[...] = h2

    logits = jnp.dot(h2.astype(BF16), wr_ref[...], preferred_element_type=F32)
    lane = lax.broadcasted_iota(jnp.int32, logits.shape, 1)
    lane_f = lane.astype(F32)
    neg = jnp.float32(-jnp.inf)
    big = jnp.float32(ROUTER_LANES)

    def first_argmax(vals):
        top = jnp.max(vals, axis=1, keepdims=True)
        idx = jnp.min(jnp.where(vals == top, lane_f, big), axis=1, keepdims=True)
        return top, idx

    g_mask = lane < N_GROUPS
    g_logits = jnp.where(g_mask, logits, neg)
    g_top, g_idx = first_argmax(g_logits)
    g_w = 1.0 / jnp.sum(jnp.where(g_mask, jnp.exp(logits - g_top), 0.0), axis=1, keepdims=True)

    e_lo = g_idx * EXPERTS_PER_GROUP + N_GROUPS
    e_mask = (lane_f >= e_lo) & (lane_f < e_lo + EXPERTS_PER_GROUP)
    e_logits = jnp.where(e_mask, logits, neg)
    v1, i1 = first_argmax(e_logits)
    v2, i2 = first_argmax(jnp.where(lane_f == i1, neg, e_logits))
    d = jnp.exp(v2 - v1)
    w1 = g_w / (1.0 + d)
    w2 = g_w * d / (1.0 + d)
    e1 = (i1 - N_GROUPS).astype(jnp.int32)
    e2 = (i2 - N_GROUPS).astype(jnp.int32)
    re_ref[...] = jnp.where(lane == 0, e1, jnp.where(lane == 1, e2, 0))
    rw_ref[...] = jnp.where(lane == 0, w1, jnp.where(lane == 1, w2, 0.0))


def _merge(oa, ob, gates, x, wba, wbb, wout, gffn, wr):
    t, d = x.shape
    tm = MERGE_TILE
    row = lambda i: (i, 0)
    const = lambda i: (0, 0)
    single = pl.Buffered(1)
    return pl.pallas_call(
        _merge_kernel,
        out_shape=(jax.ShapeDtypeStruct((t, d), F32), jax.ShapeDtypeStruct((t, d), F32),
                   jax.ShapeDtypeStruct((t, ROUTER_LANES), jnp.int32),
                   jax.ShapeDtypeStruct((t, ROUTER_LANES), F32)),
        grid=(t // tm,),
        in_specs=[pl.BlockSpec((tm, oa.shape[1]), row),
                  pl.BlockSpec((tm, ob.shape[1]), row),
                  pl.BlockSpec((tm, d), lambda i: (i, 0)),
                  pl.BlockSpec((tm, d), lambda i: (i, 1)),
                  pl.BlockSpec((tm, d), row),
                  pl.BlockSpec(wba.shape, const, pipeline_mode=single),
                  pl.BlockSpec(wbb.shape, const, pipeline_mode=single),
                  pl.BlockSpec(wout.shape, const, pipeline_mode=single),
                  pl.BlockSpec((1, d), const),
                  pl.BlockSpec(wr.shape, const, pipeline_mode=single)],
        out_specs=(pl.BlockSpec((tm, d), row), pl.BlockSpec((tm, d), row),
                   pl.BlockSpec((tm, ROUTER_LANES), row), pl.BlockSpec((tm, ROUTER_LANES), row)),
        compiler_params=_params(("parallel",)),
        name="merge_router",
    )(oa, ob, gates, gates, x, wba, wbb, wout, gffn.reshape(1, d), wr)


def _moe_kernel(be_ref, asg_ref, h_hbm, w1_ref, w3_ref, w2_ref, y_hbm, xbuf, obuf, gsem, ssem):
    del be_ref
    rows = xbuf.shape[0]

    def gather_copy(r, tok):
        return pltpu.make_async_copy(h_hbm.at[pl.ds(tok, 1), :], xbuf.at[pl.ds(r, 1), :], gsem)

    n_tokens = h_hbm.shape[0]

    def scatter_copy(r, a):
        dst = (a & 1) * n_tokens + lax.shift_right_logical(a, 1)
        return pltpu.make_async_copy(obuf.at[pl.ds(r, 1), :], y_hbm.at[pl.ds(dst, 1), :], ssem)

    def gather_start(r, c):
        a = asg_ref[0, 0, r]
        gather_copy(r, lax.shift_right_logical(jnp.maximum(a, 0), 1)).start()
        return c

    def gather_wait(r, c):
        gather_copy(r, 0).wait()
        return c

    lax.fori_loop(0, rows, gather_start, 0)
    lax.fori_loop(0, rows, gather_wait, 0)

    xb = xbuf[...].astype(BF16)
    a1 = jnp.dot(xb, w1_ref[...], preferred_element_type=F32)
    a3 = jnp.dot(xb, w3_ref[...], preferred_element_type=F32)
    u = a1 * (1.0 / (1.0 + jnp.exp(-a1))) * a3
    obuf[...] = jnp.dot(u.astype(BF16), w2_ref[...], preferred_element_type=F32)

    def scatter_start(r, c):
        a = asg_ref[0, 0, r]

        @pl.when(a >= 0)
        def _():
            scatter_copy(r, a).start()
        return c

    def scatter_wait(r, c):
        a = asg_ref[0, 0, r]

        @pl.when(a >= 0)
        def _():
            scatter_copy(r, a).wait()
        return c

    lax.fori_loop(0, rows, scatter_start, 0)
    lax.fori_loop(0, rows, scatter_wait, 0)


def _moe(block_expert, slots, h2, w1, w3, w2):
    t, d = h2.shape
    n_blocks = slots.shape[0]
    ff = w1.shape[2]
    grid_spec = pltpu.PrefetchScalarGridSpec(
        num_scalar_prefetch=1,
        grid=(n_blocks,),
        in_specs=[pl.BlockSpec((1, 1, MOE_BLOCK), lambda i, be: (i, 0, 0), memory_space=pltpu.SMEM),
                  pl.BlockSpec(memory_space=pl.ANY),
                  pl.BlockSpec((None, d, ff), lambda i, be: (be[i], 0, 0)),
                  pl.BlockSpec((None, d, ff), lambda i, be: (be[i], 0, 0)),
                  pl.BlockSpec((None, ff, d), lambda i, be: (be[i], 0, 0))],
        out_specs=pl.BlockSpec(memory_space=pl.ANY),
        scratch_shapes=[pltpu.VMEM((MOE_BLOCK, d), F32), pltpu.VMEM((MOE_BLOCK, d), F32),
                        pltpu.SemaphoreType.DMA, pltpu.SemaphoreType.DMA],
    )
    return pl.pallas_call(
        _moe_kernel,
        out_shape=jax.ShapeDtypeStruct((TOP_K * t, d), F32),
        grid_spec=grid_spec,
        compiler_params=_params(("arbitrary",)),
        name="expert_mlp",
    )(block_expert, slots, h2, w1, w3, w2)


def _routing_plan(route_e, n_tokens):
    n_assign = n_tokens * TOP_K
    e_flat = route_e.reshape(n_assign)
    order = jnp.argsort(e_flat).astype(jnp.int32)
    counts = jnp.zeros((N_EXPERTS,), jnp.int32).at[e_flat].add(1)
    offsets = jnp.cumsum(counts) - counts
    blocks_per_expert = (counts + MOE_BLOCK - 1) // MOE_BLOCK
    block_ends = jnp.cumsum(blocks_per_expert)
    n_blocks = n_assign // MOE_BLOCK + N_EXPERTS
    block_id = jnp.arange(n_blocks, dtype=jnp.int32)
    block_expert = jnp.clip(jnp.searchsorted(block_ends, block_id, side="right"), 0, N_EXPERTS - 1).astype(jnp.int32)
    first_block = (block_ends - blocks_per_expert)[block_expert]
    rank = (block_id - first_block)[:, None] * MOE_BLOCK + jnp.arange(MOE_BLOCK, dtype=jnp.int32)[None, :]
    valid = (rank < counts[block_expert][:, None]) & (block_id < block_ends[-1])[:, None]
    src = jnp.clip(offsets[block_expert][:, None] + rank, 0, n_assign - 1)
    slots = jnp.where(valid, order[src], -1).astype(jnp.int32)
    return block_expert, slots.reshape(n_blocks, 1, MOE_BLOCK)


def _final_kernel(x1_ref, y0_ref, y1_ref, rw_ref, g_ref, o_ref, *, normalize):
    rw = rw_ref[...]
    lane = lax.broadcasted_iota(jnp.int32, rw.shape, 1)
    w1 = jnp.sum(jnp.where(lane == 0, rw, 0.0), axis=1, keepdims=True)
    w2 = jnp.sum(jnp.where(lane == 1, rw, 0.0), axis=1, keepdims=True)
    x2 = x1_ref[...] + (y0_ref[...] * w1 + y1_ref[...] * w2)
    if normalize:
        ms = jnp.mean(x2 * x2, axis=-1, keepdims=True)
        x2 = x2 * lax.rsqrt(ms + EPS) * g_ref[...]
    o_ref[...] = x2


def _final(x1, y2, route_w, g, *, normalize):
    t, d = x1.shape
    tm = MERGE_TILE
    nt = t // tm
    return pl.pallas_call(
        functools.partial(_final_kernel, normalize=normalize),
        out_shape=jax.ShapeDtypeStruct((t, d), F32),
        grid=(nt,),
        in_specs=[pl.BlockSpec((tm, d), lambda i: (i, 0)),
                  pl.BlockSpec((tm, d), lambda i: (i, 0)),
                  pl.BlockSpec((tm, d), lambda i: (i + nt, 0)),
                  pl.BlockSpec((tm, ROUTER_LANES), lambda i: (i, 0)),
                  pl.BlockSpec((1, d), lambda i: (0, 0))],
        out_specs=pl.BlockSpec((tm, d), lambda i: (i, 0)),
        compiler_params=_params(("parallel",)),
        name="combine_norm",
    )(x1, y2, y2, route_w, g.reshape(1, d))


def _rope_tables(seq):
    pos = jnp.arange(seq, dtype=jnp.int32)
    inv = ROPE_THETA ** (-jnp.arange(0, 2 * ROPE_HALF, 2, dtype=F32) / (2 * ROPE_HALF))

    def cs(p):
        ang = p.astype(F32)[:, None] * inv[None, :]
        return jnp.cos(ang), jnp.sin(ang)

    cr, sr = cs(pos // GRID_W)
    cc, sc = cs(pos % GRID_W)
    cp, sp = cs(pos)
    axial = (jnp.concatenate([cr, cr, cc, cc], axis=1), jnp.concatenate([-sr, sr, -sc, sc], axis=1))
    linear = (jnp.concatenate([cp, cp, cp, cp], axis=1), jnp.concatenate([-sp, sp, -sp, sp], axis=1))
    return axial, linear


def kernel(x, g_mix, w_in, q_norm_a, k_norm_a, lam_q1, lam_k1, lam_q2, lam_k2, subln_b, w_branch_a,
           w_branch_b, w_gate, b_gate, w_out, g_ffn, w_router_group, w_router_expert, w_e_gate, w_e_up,
           w_e_down, g_final):
    batch, seq, d = x.shape
    t = batch * seq
    depth = g_mix.shape[0]
    a_q = A_HEADS * HEAD
    a_kv = A_KV_HEADS * HEAD
    b_w = B_HEADS * HEAD
    cuts = [0, a_q, a_q + a_kv, a_q + 2 * a_kv, a_q + 2 * a_kv + b_w, a_q + 2 * a_kv + 2 * b_w,
            a_q + 2 * a_kv + 3 * b_w]
    (cos_a, sin_a), (cos_b, sin_b) = _rope_tables(seq)
    scale_a = HEAD ** -0.5
    scale_b = B_QK_DIM ** -0.5
    ones = jnp.ones((HEAD,), F32)

    xf = x.reshape(t, d)
    for l in range(depth):
        lam_init = 0.8 - 0.6 * math.exp(-0.3 * l)
        w = w_in[l].astype(BF16)
        w_qa, w_ka, w_va, w_qb, w_kb, w_vb = (w[:, cuts[i]:cuts[i + 1]] for i in range(6))

        h = _rmsnorm(xf, g_mix[l], BF16)
        qa = _proj_rope(h, w_qa, cos_a * scale_a, sin_a * scale_a, q_norm_a[l], normalize=True, seq=seq)
        ka = _proj_rope(h, w_ka, cos_a, sin_a, k_norm_a[l], normalize=True, seq=seq)
        vat = _proj_t(h, w_va.T)
        qb = _proj_rope(h, w_qb, cos_b * scale_b, sin_b * scale_b, ones, normalize=False, seq=seq)
        kb = _proj_rope(h, w_kb, cos_b, sin_b, ones, normalize=False, seq=seq)
        vbt = _proj_t(h, w_vb.T)
        gates = _proj_gate(h, w_gate[l].astype(BF16), b_gate[l])

        o_a = _attention(qa, ka, vat, [], batch=batch, seq=seq, kv_heads=A_KV_HEADS, q_cols=A_GROUP * HEAD,
                         differential=False, lam_init=lam_init)
        lam_rows = [v[l].reshape(1, B_QK_DIM) for v in (lam_q1, lam_k1, lam_q2, lam_k2)]
        o_b = _attention(qb, kb, vbt, lam_rows + [subln_b[l].reshape(HEAD, 1)], batch=batch, seq=seq,
                         kv_heads=B_HEADS, q_cols=HEAD, differential=True, lam_init=lam_init)

        w_r = jnp.zeros((d, ROUTER_LANES), F32)
        w_r = w_r.at[:, :N_GROUPS].set(w_router_group[l]).at[:, N_GROUPS:N_GROUPS + N_EXPERTS].set(w_router_expert[l])
        x1, h2, route_e, route_w = _merge(o_a, o_b, gates, xf, w_branch_a[l].astype(BF16),
                                          w_branch_b[l].astype(BF16), w_out[l].astype(BF16), g_ffn[l],
                                          w_r.astype(BF16))

        block_expert, slots = _routing_plan(route_e[:, :TOP_K], t)
        y2 = _moe(block_expert, slots, h2, w_e_gate[l].astype(BF16), w_e_up[l].astype(BF16),
                  w_e_down[l].astype(BF16))
        xf = _final(x1, y2, route_w, g_final, normalize=(l + 1 == depth))
    return xf.reshape(batch, seq, d)
```

```python
import functools
import math

import jax
import jax.numpy as jnp
from jax import lax
from jax.experimental import pallas as pl
from jax.experimental.pallas import tpu as pltpu

F32 = jnp.float32
BF16 = jnp.bfloat16

GRID_W = 64
ROPE_THETA = 10000.0
EPS = 1e-6
HEAD = 128
ROPE_HALF = 32
A_HEADS = 8
A_KV_HEADS = 2
A_GROUP = A_HEADS // A_KV_HEADS
B_HEADS = 8
B_QK_DIM = 64
N_GROUPS = 4
EXPERTS_PER_GROUP = 8
N_EXPERTS = N_GROUPS * EXPERTS_PER_GROUP
TOP_K = 2
ROUTER_LANES = 128

VMEM_LIMIT = 56 * 1024 * 1024

ROW_TILE = 512
MERGE_TILE = 256
SCORE_LANES = 1024
KV_CHUNK = 512
MOE_BLOCK = 256


def _params(sem):
    return pltpu.CompilerParams(dimension_semantics=sem, vmem_limit_bytes=VMEM_LIMIT)


def _rmsnorm_kernel(x_ref, g_ref, o_ref):
    x = x_ref[...]
    ms = jnp.mean(x * x, axis=-1, keepdims=True)
    o_ref[...] = (x * lax.rsqrt(ms + EPS) * g_ref[...]).astype(o_ref.dtype)


def _rmsnorm(x, g, out_dtype):
    t, d = x.shape
    return pl.pallas_call(
        _rmsnorm_kernel,
        out_shape=jax.ShapeDtypeStruct((t, d), out_dtype),
        grid=(t // ROW_TILE,),
        in_specs=[pl.BlockSpec((ROW_TILE, d), lambda i: (i, 0)),
                  pl.BlockSpec((1, d), lambda i: (0, 0))],
        out_specs=pl.BlockSpec((ROW_TILE, d), lambda i: (i, 0)),
        compiler_params=_params(("parallel",)),
        name="rmsnorm",
    )(x, g.reshape(1, d))


def _rotary(t, cos, sin):
    lane = lax.broadcasted_iota(jnp.int32, t.shape, 1)
    first = (lane % (2 * ROPE_HALF)) < ROPE_HALF
    partner = jnp.where(first, pltpu.roll(t, HEAD - ROPE_HALF, 1), pltpu.roll(t, ROPE_HALF, 1))
    return t * cos + partner * sin


def _proj_rope_kernel(h_ref, w_ref, cos_ref, sin_ref, g_ref, o_ref, *, normalize):
    acc = jnp.dot(h_ref[...], w_ref[...], preferred_element_type=F32)
    cos = cos_ref[...]
    sin = sin_ref[...]
    for c in range(acc.shape[1] // HEAD):
        t = acc[:, c * HEAD:(c + 1) * HEAD]
        if normalize:
            ms = jnp.mean(t * t, axis=-1, keepdims=True)
            t = t * lax.rsqrt(ms + EPS) * g_ref[...]
        o_ref[:, c * HEAD:(c + 1) * HEAD] = _rotary(t, cos, sin).astype(o_ref.dtype)


def _proj_rope(h, w, cos, sin, gain, *, normalize, seq):
    t, d = h.shape
    n = w.shape[1]
    tiles_per_seq = seq // ROW_TILE
    return pl.pallas_call(
        functools.partial(_proj_rope_kernel, normalize=normalize),
        out_shape=jax.ShapeDtypeStruct((t, n), BF16),
        grid=(t // ROW_TILE,),
        in_specs=[pl.BlockSpec((ROW_TILE, d), lambda i: (i, 0)),
                  pl.BlockSpec((d, n), lambda i: (0, 0)),
                  pl.BlockSpec((ROW_TILE, HEAD), lambda i: (i % tiles_per_seq, 0)),
                  pl.BlockSpec((ROW_TILE, HEAD), lambda i: (i % tiles_per_seq, 0)),
                  pl.BlockSpec((1, HEAD), lambda i: (0, 0))],
        out_specs=pl.BlockSpec((ROW_TILE, n), lambda i: (i, 0)),
        compiler_params=_params(("parallel",)),
        name="proj_rope",
    )(h, w, cos, sin, gain.reshape(1, HEAD))


def _proj_t_kernel(h_ref, wt_ref, o_ref):
    o_ref[...] = lax.dot_general(wt_ref[...], h_ref[...], (((1,), (1,)), ((), ())),
                                 preferred_element_type=F32).astype(o_ref.dtype)


def _proj_t(h, wt):
    t, d = h.shape
    n = wt.shape[0]
    return pl.pallas_call(
        _proj_t_kernel,
        out_shape=jax.ShapeDtypeStruct((n, t), BF16),
        grid=(t // ROW_TILE,),
        in_specs=[pl.BlockSpec((ROW_TILE, d), lambda i: (i, 0)),
                  pl.BlockSpec((n, d), lambda i: (0, 0))],
        out_specs=pl.BlockSpec((n, ROW_TILE), lambda i: (0, i)),
        compiler_params=_params(("parallel",)),
        name="proj_t",
    )(h, wt)


def _proj_gate_kernel(h_ref, w_ref, b_ref, o_ref):
    z = jnp.dot(h_ref[...], w_ref[...], preferred_element_type=F32) + b_ref[...]
    o_ref[...] = (1.0 / (1.0 + jnp.exp(-z))).astype(o_ref.dtype)


def _proj_gate(h, w, b):
    t, d = h.shape
    n = w.shape[1]
    tn = 1024
    return pl.pallas_call(
        _proj_gate_kernel,
        out_shape=jax.ShapeDtypeStruct((t, n), BF16),
        grid=(n // tn, t // ROW_TILE),
        in_specs=[pl.BlockSpec((ROW_TILE, d), lambda j, i: (i, 0)),
                  pl.BlockSpec((d, tn), lambda j, i: (0, j)),
                  pl.BlockSpec((1, tn), lambda j, i: (0, j))],
        out_specs=pl.BlockSpec((ROW_TILE, tn), lambda j, i: (i, j)),
        compiler_params=_params(("parallel", "parallel")),
        name="proj_gate",
    )(h, w, b.reshape(1, n))


def _attn_kernel(*refs, differential, lam_init):
    if differential:
        (q_ref, k_ref, vt_ref, lq1_ref, lk1_ref, lq2_ref, lk2_ref, sub_ref, o_ref,
         q_sc, st_sc, m_sc, l_sc, acc_sc) = refs
        q = q_ref[...]
        lane = lax.broadcasted_iota(jnp.int32, q.shape, 1)
        zero = jnp.zeros_like(q)
        qs = [jnp.where(lane < B_QK_DIM, q, zero), jnp.where(lane >= B_QK_DIM, q, zero)]
    else:
        q_ref, k_ref, vt_ref, o_ref, q_sc, st_sc, m_sc, l_sc, acc_sc = refs
        qs = [q_ref[:, g * HEAD:(g + 1) * HEAD] for g in range(q_ref.shape[1] // HEAD)]
    tq = q_ref.shape[0]
    n_chunks = k_ref.shape[0] // KV_CHUNK

    for g, qg in enumerate(qs):
        q_sc[g * tq:(g + 1) * tq, :] = qg
    m_sc[...] = jnp.full(m_sc.shape, -jnp.inf, F32)
    l_sc[...] = jnp.zeros(l_sc.shape, F32)
    acc_sc[...] = jnp.zeros(acc_sc.shape, F32)

    def scores(j, slot):
        off = pl.multiple_of(j * KV_CHUNK, KV_CHUNK)
        st_sc[slot] = lax.dot_general(k_ref[pl.ds(off, KV_CHUNK), :], q_sc[...], (((1,), (1,)), ((), ())),
                                      preferred_element_type=F32)

    def consume(j, slot):
        off = pl.multiple_of(j * KV_CHUNK, KV_CHUNK)
        st = st_sc[slot]
        m_old = m_sc[...]
        m_new = jnp.maximum(m_old, jnp.max(st, axis=0, keepdims=True))
        alpha = jnp.exp2(m_old - m_new)
        p = jnp.exp2(st - m_new)
        l_sc[...] = alpha * l_sc[...] + jnp.sum(p, axis=0, keepdims=True)
        acc_sc[...] = alpha * acc_sc[...] + jnp.dot(vt_ref[:, pl.ds(off, KV_CHUNK)], p.astype(BF16),
                                                    preferred_element_type=F32)
        m_sc[...] = m_new

    scores(0, 0)

    def pair(i, c):
        j = 2 * i
        scores(j + 1, 1)
        consume(j, 0)
        scores(j + 2, 0)
        consume(j + 1, 1)
        return c

    lax.fori_loop(0, n_chunks // 2 - 1, pair, 0)
    scores(n_chunks - 1, 1)
    consume(n_chunks - 2, 0)
    consume(n_chunks - 1, 1)

    out = acc_sc[...] / l_sc[...]
    if differential:
        lam = (jnp.exp(jnp.sum(lq1_ref[...] * lk1_ref[...], axis=1, keepdims=True))
               - jnp.exp(jnp.sum(lq2_ref[...] * lk2_ref[...], axis=1, keepdims=True)) + lam_init)
        ot = out[:, :tq] - lam * out[:, tq:]
        ms = jnp.mean(ot * ot, axis=0, keepdims=True)
        ot = ot * lax.rsqrt(ms + EPS) * sub_ref[...] * (1.0 - lam_init)
        o_ref[...] = ot.T.astype(o_ref.dtype)
    else:
        for g in range(len(qs)):
            o_ref[:, g * HEAD:(g + 1) * HEAD] = out[:, g * tq:(g + 1) * tq].T.astype(o_ref.dtype)


def _attention(q, k, vt, extra, *, batch, seq, kv_heads, q_cols, differential, lam_init):
    t = q.shape[0]
    assert (seq // KV_CHUNK) % 2 == 0 and seq // KV_CHUNK >= 4
    n_maps = 2 if differential else q_cols // HEAD
    tq = SCORE_LANES // n_maps
    nq = seq // tq
    in_specs = [pl.BlockSpec((tq, q_cols), lambda b, h, i: (b * nq + i, h)),
                pl.BlockSpec((seq, HEAD), lambda b, h, i: (b, h)),
                pl.BlockSpec((HEAD, seq), lambda b, h, i: (h, b))]
    in_specs += [pl.BlockSpec(e.shape, lambda b, h, i: (0, 0)) for e in extra]
    return pl.pallas_call(
        functools.partial(_attn_kernel, differential=differential, lam_init=lam_init),
        out_shape=jax.ShapeDtypeStruct((t, kv_heads * q_cols), BF16),
        grid=(batch, kv_heads, nq),
        in_specs=in_specs,
        out_specs=pl.BlockSpec((tq, q_cols), lambda b, h, i: (b * nq + i, h)),
        scratch_shapes=[pltpu.VMEM((SCORE_LANES, HEAD), BF16),
                        pltpu.VMEM((2, KV_CHUNK, SCORE_LANES), F32),
                        pltpu.VMEM((1, SCORE_LANES), F32), pltpu.VMEM((1, SCORE_LANES), F32),
                        pltpu.VMEM((HEAD, SCORE_LANES), F32)],
        compiler_params=_params(("parallel", "parallel", "parallel")),
        name="diff_attention" if differential else "axial_gqa",
    )(q, k, vt, *extra)


def _merge_kernel(oa_ref, ob_ref, ga_ref, gb_ref, x_ref, wba_ref, wbb_ref, wout_ref, gffn_ref, wr_ref,
                  x1_ref, h2_ref, re_ref, rw_ref):
    pa = jnp.dot(oa_ref[...], wba_ref[...], preferred_element_type=F32)
    pb = jnp.dot(ob_ref[...], wbb_ref[...], preferred_element_type=F32)
    merged = ga_ref[...].astype(F32) * pa + gb_ref[...].astype(F32) * pb
    x1 = x_ref[...] + jnp.dot(merged.astype(BF16), wout_ref[...], preferred_element_type=F32)
    x1_ref[...] = x1
    ms = jnp.mean(x1 * x1, axis=-1, keepdims=True)
    h2 = x1 * lax.rsqrt(ms + EPS) * gffn_ref[...]
    h2_ref[...] = h2

    logits = jnp.dot(h2.astype(BF16), wr_ref[...], preferred_element_type=F32)
    lane = lax.broadcasted_iota(jnp.int32, logits.shape, 1)
    lane_f = lane.astype(F32)
    neg = jnp.float32(-jnp.inf)
    big = jnp.float32(ROUTER_LANES)

    def first_argmax(vals):
        top = jnp.max(vals, axis=1, keepdims=True)
        idx = jnp.min(jnp.where(vals == top, lane_f, big), axis=1, keepdims=True)
        return top, idx

    g_mask = lane < N_GROUPS
    g_logits = jnp.where(g_mask, logits, neg)
    g_top, g_idx = first_argmax(g_logits)
    g_w = 1.0 / jnp.sum(jnp.where(g_mask, jnp.exp(logits - g_top), 0.0), axis=1, keepdims=True)

    e_lo = g_idx * EXPERTS_PER_GROUP + N_GROUPS
    e_mask = (lane_f >= e_lo) & (lane_f < e_lo + EXPERTS_PER_GROUP)
    e_logits = jnp.where(e_mask, logits, neg)
    v1, i1 = first_argmax(e_logits)
    v2, i2 = first_argmax(jnp.where(lane_f == i1, neg, e_logits))
    d = jnp.exp(v2 - v1)
    w1 = g_w / (1.0 + d)
    w2 = g_w * d / (1.0 + d)
    e1 = (i1 - N_GROUPS).astype(jnp.int32)
    e2 = (i2 - N_GROUPS).astype(jnp.int32)
    re_ref[...] = jnp.where(lane == 0, e1, jnp.where(lane == 1, e2, 0))
    rw_ref[...] = jnp.where(lane == 0, w1, jnp.where(lane == 1, w2, 0.0))


def _merge(oa, ob, gates, x, wba, wbb, wout, gffn, wr):
    t, d = x.shape
    tm = MERGE_TILE
    row = lambda i: (i, 0)
    const = lambda i: (0, 0)
    single = pl.Buffered(1)
    return pl.pallas_call(
        _merge_kernel,
        out_shape=(jax.ShapeDtypeStruct((t, d), F32), jax.ShapeDtypeStruct((t, d), F32),
                   jax.ShapeDtypeStruct((t, ROUTER_LANES), jnp.int32),
                   jax.ShapeDtypeStruct((t, ROUTER_LANES), F32)),
        grid=(t // tm,),
        in_specs=[pl.BlockSpec((tm, oa.shape[1]), row),
                  pl.BlockSpec((tm, ob.shape[1]), row),
                  pl.BlockSpec((tm, d), lambda i: (i, 0)),
                  pl.BlockSpec((tm, d), lambda i: (i, 1)),
                  pl.BlockSpec((tm, d), row),
                  pl.BlockSpec(wba.shape, const, pipeline_mode=single),
                  pl.BlockSpec(wbb.shape, const, pipeline_mode=single),
                  pl.BlockSpec(wout.shape, const, pipeline_mode=single),
                  pl.BlockSpec((1, d), const),
                  pl.BlockSpec(wr.shape, const, pipeline_mode=single)],
        out_specs=(pl.BlockSpec((tm, d), row), pl.BlockSpec((tm, d), row),
                   pl.BlockSpec((tm, ROUTER_LANES), row), pl.BlockSpec((tm, ROUTER_LANES), row)),
        compiler_params=_params(("parallel",)),
        name="merge_router",
    )(oa, ob, gates, gates, x, wba, wbb, wout, gffn.reshape(1, d), wr)


def _moe_kernel(be_ref, asg_ref, h_hbm, w1_ref, w3_ref, w2_ref, y_hbm, xbuf, obuf, gsem, ssem):
    del be_ref
    rows = xbuf.shape[0]

    def gather_copy(r, tok):
        return pltpu.make_async_copy(h_hbm.at[pl.ds(tok, 1), :], xbuf.at[pl.ds(r, 1), :], gsem)

    n_tokens = h_hbm.shape[0]

    def scatter_copy(r, a):
        dst = (a & 1) * n_tokens + lax.shift_right_logical(a, 1)
        return pltpu.make_async_copy(obuf.at[pl.ds(r, 1), :], y_hbm.at[pl.ds(dst, 1), :], ssem)

    def gather_start(r, c):
        a = asg_ref[0, 0, r]
        gather_copy(r, lax.shift_right_logical(jnp.maximum(a, 0), 1)).start()
        return c

    def gather_wait(r, c):
        gather_copy(r, 0).wait()
        return c

    lax.fori_loop(0, rows, gather_start, 0)
    lax.fori_loop(0, rows, gather_wait, 0)

    xb = xbuf[...].astype(BF16)
    a1 = jnp.dot(xb, w1_ref[...], preferred_element_type=F32)
    a3 = jnp.dot(xb, w3_ref[...], preferred_element_type=F32)
    u = a1 * (1.0 / (1.0 + jnp.exp(-a1))) * a3
    obuf[...] = jnp.dot(u.astype(BF16), w2_ref[...], preferred_element_type=F32)

    def scatter_start(r, c):
        a = asg_ref[0, 0, r]

        @pl.when(a >= 0)
        def _():
            scatter_copy(r, a).start()
        return c

    def scatter_wait(r, c):
        a = asg_ref[0, 0, r]

        @pl.when(a >= 0)
        def _():
            scatter_copy(r, a).wait()
        return c

    lax.fori_loop(0, rows, scatter_start, 0)
    lax.fori_loop(0, rows, scatter_wait, 0)


def _moe(block_expert, slots, h2, w1, w3, w2):
    t, d = h2.shape
    n_blocks = slots.shape[0]
    ff = w1.shape[2]
    grid_spec = pltpu.PrefetchScalarGridSpec(
        num_scalar_prefetch=1,
        grid=(n_blocks,),
        in_specs=[pl.BlockSpec((1, 1, MOE_BLOCK), lambda i, be: (i, 0, 0), memory_space=pltpu.SMEM),
                  pl.BlockSpec(memory_space=pl.ANY),
                  pl.BlockSpec((None, d, ff), lambda i, be: (be[i], 0, 0)),
                  pl.BlockSpec((None, d, ff), lambda i, be: (be[i], 0, 0)),
                  pl.BlockSpec((None, ff, d), lambda i, be: (be[i], 0, 0))],
        out_specs=pl.BlockSpec(memory_space=pl.ANY),
        scratch_shapes=[pltpu.VMEM((MOE_BLOCK, d), F32), pltpu.VMEM((MOE_BLOCK, d), F32),
                        pltpu.SemaphoreType.DMA, pltpu.SemaphoreType.DMA],
    )
    return pl.pallas_call(
        _moe_kernel,
        out_shape=jax.ShapeDtypeStruct((TOP_K * t, d), F32),
        grid_spec=grid_spec,
        compiler_params=_params(("arbitrary",)),
        name="expert_mlp",
    )(block_expert, slots, h2, w1, w3, w2)


def _routing_plan(route_e, n_tokens):
    n_assign = n_tokens * TOP_K
    e_flat = route_e.reshape(n_assign)
    order = jnp.argsort(e_flat).astype(jnp.int32)
    counts = jnp.zeros((N_EXPERTS,), jnp.int32).at[e_flat].add(1)
    offsets = jnp.cumsum(counts) - counts
    blocks_per_expert = (counts + MOE_BLOCK - 1) // MOE_BLOCK
    block_ends = jnp.cumsum(blocks_per_expert)
    n_blocks = n_assign // MOE_BLOCK + N_EXPERTS
    block_id = jnp.arange(n_blocks, dtype=jnp.int32)
    block_expert = jnp.clip(jnp.searchsorted(block_ends, block_id, side="right"), 0, N_EXPERTS - 1).astype(jnp.int32)
    first_block = (block_ends - blocks_per_expert)[block_expert]
    rank = (block_id - first_block)[:, None] * MOE_BLOCK + jnp.arange(MOE_BLOCK, dtype=jnp.int32)[None, :]
    valid = (rank < counts[block_expert][:, None]) & (block_id < block_ends[-1])[:, None]
    src = jnp.clip(offsets[block_expert][:, None] + rank, 0, n_assign - 1)
    slots = jnp.where(valid, order[src], -1).astype(jnp.int32)
    return block_expert, slots.reshape(n_blocks, 1, MOE_BLOCK)


def _final_kernel(x1_ref, y0_ref, y1_ref, rw_ref, g_ref, o_ref, *, normalize):
    rw = rw_ref[...]
    lane = lax.broadcasted_iota(jnp.int32, rw.shape, 1)
    w1 = jnp.sum(jnp.where(lane == 0, rw, 0.0), axis=1, keepdims=True)
    w2 = jnp.sum(jnp.where(lane == 1, rw, 0.0), axis=1, keepdims=True)
    x2 = x1_ref[...] + (y0_ref[...] * w1 + y1_ref[...] * w2)
    if normalize:
        ms = jnp.mean(x2 * x2, axis=-1, keepdims=True)
        x2 = x2 * lax.rsqrt(ms + EPS) * g_ref[...]
    o_ref[...] = x2


def _final(x1, y2, route_w, g, *, normalize):
    t, d = x1.shape
    tm = MERGE_TILE
    nt = t // tm
    return pl.pallas_call(
        functools.partial(_final_kernel, normalize=normalize),
        out_shape=jax.ShapeDtypeStruct((t, d), F32),
        grid=(nt,),
        in_specs=[pl.BlockSpec((tm, d), lambda i: (i, 0)),
                  pl.BlockSpec((tm, d), lambda i: (i, 0)),
                  pl.BlockSpec((tm, d), lambda i: (i + nt, 0)),
                  pl.BlockSpec((tm, ROUTER_LANES), lambda i: (i, 0)),
                  pl.BlockSpec((1, d), lambda i: (0, 0))],
        out_specs=pl.BlockSpec((tm, d), lambda i: (i, 0)),
        compiler_params=_params(("parallel",)),
        name="combine_norm",
    )(x1, y2, y2, route_w, g.reshape(1, d))


def _rope_tables(seq):
    pos = jnp.arange(seq, dtype=jnp.int32)
    inv = ROPE_THETA ** (-jnp.arange(0, 2 * ROPE_HALF, 2, dtype=F32) / (2 * ROPE_HALF))

    def cs(p):
        ang = p.astype(F32)[:, None] * inv[None, :]
        return jnp.cos(ang), jnp.sin(ang)

    cr, sr = cs(pos // GRID_W)
    cc, sc = cs(pos % GRID_W)
    cp, sp = cs(pos)
    axial = (jnp.concatenate([cr, cr, cc, cc], axis=1), jnp.concatenate([-sr, sr, -sc, sc], axis=1))
    linear = (jnp.concatenate([cp, cp, cp, cp], axis=1), jnp.concatenate([-sp, sp, -sp, sp], axis=1))
    return axial, linear


def kernel(x, g_mix, w_in, q_norm_a, k_norm_a, lam_q1, lam_k1, lam_q2, lam_k2, subln_b, w_branch_a,
           w_branch_b, w_gate, b_gate, w_out, g_ffn, w_router_group, w_router_expert, w_e_gate, w_e_up,
           w_e_down, g_final):
    batch, seq, d = x.shape
    t = batch * seq
    depth = g_mix.shape[0]
    a_q = A_HEADS * HEAD
    a_kv = A_KV_HEADS * HEAD
    b_w = B_HEADS * HEAD
    cuts = [0, a_q, a_q + a_kv, a_q + 2 * a_kv, a_q + 2 * a_kv + b_w, a_q + 2 * a_kv + 2 * b_w,
            a_q + 2 * a_kv + 3 * b_w]
    (cos_a, sin_a), (cos_b, sin_b) = _rope_tables(seq)
    scale_a = HEAD ** -0.5 * math.log2(math.e)
    scale_b = B_QK_DIM ** -0.5 * math.log2(math.e)
    ones = jnp.ones((HEAD,), F32)

    xf = x.reshape(t, d)
    for l in range(depth):
        lam_init = 0.8 - 0.6 * math.exp(-0.3 * l)
        w = w_in[l].astype(BF16)
        w_qa, w_ka, w_va, w_qb, w_kb, w_vb = (w[:, cuts[i]:cuts[i + 1]] for i in range(6))

        h = _rmsnorm(xf, g_mix[l], BF16)
        qa =_proj_rope(h, w_qa, cos_a * scale_a, sin_a * scale_a, q_norm_a[l], normalize=True, seq=seq)
        ka = _proj_rope(h, w_ka, cos_a, sin_a, k_norm_a[l], normalize=True, seq=seq)
        vat = _proj_t(h, w_va.T)
        qb = _proj_rope(h, w_qb, cos_b * scale_b, sin_b * scale_b, ones, normalize=False, seq=seq)
        kb = _proj_rope(h, w_kb, cos_b, sin_b, ones, normalize=False, seq=seq)
        vbt = _proj_t(h, w_vb.T)
        gates = _proj_gate(h, w_gate[l].astype(BF16), b_gate[l])

        o_a = _attention(qa, ka, vat, [], batch=batch, seq=seq, kv_heads=A_KV_HEADS, q_cols=A_GROUP * HEAD,
                         differential=False, lam_init=lam_init)
        lam_rows = [v[l].reshape(1, B_QK_DIM) for v in (lam_q1, lam_k1, lam_q2, lam_k2)]
        o_b = _attention(qb, kb, vbt, lam_rows + [subln_b[l].reshape(HEAD, 1)], batch=batch, seq=seq,
                         kv_heads=B_HEADS, q_cols=HEAD, differential=True, lam_init=lam_init)

        w_r = jnp.zeros((d, ROUTER_LANES), F32)
        w_r = w_r.at[:, :N_GROUPS].set(w_router_group[l]).at[:, N_GROUPS:N_GROUPS + N_EXPERTS].set(w_router_expert[l])
        x1, h2, route_e, route_w = _merge(o_a, o_b, gates, xf, w_branch_a[l].astype(BF16),
                                          w_branch_b[l].astype(BF16), w_out[l].astype(BF16), g_ffn[l],
                                          w_r.astype(BF16))

        block_expert, slots = _routing_plan(route_e[:, :TOP_K], t)
        y2 = _moe(block_expert, slots, h2, w_e_gate[l].astype(BF16), w_e_up[l].astype(BF16),
                  w_e_down[l].astype(BF16))
        xf = _final(x1, y2, route_w, g_final, normalize=(l + 1 == depth))
    return xf.reshape(batch, seq, d)
```

```python
import functools
import math

import jax
import jax.numpy as jnp
from jax import lax
from jax.experimental import pallas as pl
from jax.experimental.pallas import tpu as pltpu

F32 = jnp.float32
BF16 = jnp.bfloat16

GRID_W = 64
ROPE_THETA = 10000.0
EPS = 1e-6
LANES = 128
HEAD = 128
V_ROWS = HEAD + 16
ROPE_HALF = 32
A_HEADS = 8
A_KV_HEADS = 2
A_GROUP = A_HEADS // A_KV_HEADS
B_HEADS = 8
B_QK_DIM = 64
N_GROUPS = 4
EXPERTS_PER_GROUP = 8
N_EXPERTS = N_GROUPS * EXPERTS_PER_GROUP
TOP_K = 2
ROUTER_LANES = 128

VMEM_LIMIT = 56 * 1024 * 1024

ROW_TILE = 512
MERGE_TILE = 256
SCORE_LANES = 1024
KV_CHUNK = 512
SCORE_SLOTS = 3
MOE_BLOCK = 256
DMA_UNROLL = 8


def _params(sem):
    return pltpu.CompilerParams(dimension_semantics=sem, vmem_limit_bytes=VMEM_LIMIT)


def _slab_pitch(d):
    return d // LANES + 1


def _slab_store(ref, val):
    rows, d = val.shape
    pitch = _slab_pitch(d)
    for s in range(d // LANES):
        ref[pl.ds(s, rows, stride=pitch), :] = val[:, s * LANES:(s + 1) * LANES]
    ref[pl.ds(d // LANES, rows, stride=pitch), :] = jnp.zeros((rows, LANES), val.dtype)


def _slab_load(ref, rows, d):
    pitch = _slab_pitch(d)
    return jnp.concatenate([ref[pl.ds(s, rows, stride=pitch), :] for s in range(d // LANES)], axis=1)


def _rmsnorm_kernel(x_ref, g_ref, o_ref):
    x = x_ref[...]
    ms = jnp.mean(x * x, axis=-1, keepdims=True)
    o_ref[...] = (x * lax.rsqrt(ms + EPS) * g_ref[...]).astype(o_ref.dtype)


def _rmsnorm(x, g, out_dtype):
    t, d = x.shape
    return pl.pallas_call(
        _rmsnorm_kernel,
        out_shape=jax.ShapeDtypeStruct((t, d), out_dtype),
        grid=(t // ROW_TILE,),
        in_specs=[pl.BlockSpec((ROW_TILE, d), lambda i: (i, 0)),
                  pl.BlockSpec((1, d), lambda i: (0, 0))],
        out_specs=pl.BlockSpec((ROW_TILE, d), lambda i: (i, 0)),
        compiler_params=_params(("parallel",)),
        name="rmsnorm",
    )(x, g.reshape(1, d))


def _rotary(t, cos, sin):
    lane = lax.broadcasted_iota(jnp.int32, t.shape, 1)
    first = (lane % (2 * ROPE_HALF)) < ROPE_HALF
    partner = jnp.where(first, pltpu.roll(t, HEAD - ROPE_HALF, 1), pltpu.roll(t, ROPE_HALF, 1))
    return t * cos + partner * sin


def _proj_rope_kernel(h_ref, w_ref, cos_ref, sin_ref, g_ref, o_ref, *, normalize):
    acc = jnp.dot(h_ref[...], w_ref[...], preferred_element_type=F32)
    cos = cos_ref[...]
    sin = sin_ref[...]
    for c in range(acc.shape[1] // HEAD):
        t = acc[:, c * HEAD:(c + 1) * HEAD]
        if normalize:
            ms = jnp.mean(t * t, axis=-1, keepdims=True)
            t = t * lax.rsqrt(ms + EPS) * g_ref[...]
        o_ref[:, c * HEAD:(c + 1) * HEAD] = _rotary(t, cos, sin).astype(o_ref.dtype)


def _proj_rope(h, w, cos, sin, gain, *, normalize, seq):
    t, d = h.shape
    n = w.shape[1]
    tiles_per_seq = seq // ROW_TILE
    return pl.pallas_call(
        functools.partial(_proj_rope_kernel, normalize=normalize),
        out_shape=jax.ShapeDtypeStruct((t, n), BF16),
        grid=(t // ROW_TILE,),
        in_specs=[pl.BlockSpec((ROW_TILE, d), lambda i: (i, 0)),
                  pl.BlockSpec((d, n), lambda i: (0, 0)),
                  pl.BlockSpec((ROW_TILE, HEAD), lambda i: (i % tiles_per_seq, 0)),
                  pl.BlockSpec((ROW_TILE, HEAD), lambda i: (i % tiles_per_seq, 0)),
                  pl.BlockSpec((1, HEAD), lambda i: (0, 0))],
        out_specs=pl.BlockSpec((ROW_TILE, n), lambda i: (i, 0)),
        compiler_params=_params(("parallel",)),
        name="proj_rope",
    )(h, w, cos, sin, gain.reshape(1, HEAD))


def _proj_t_kernel(h_ref, wt_ref, o_ref):
    res = lax.dot_general(wt_ref[...], h_ref[...], (((1,), (1,)), ((), ())), preferred_element_type=F32)
    pad = V_ROWS - HEAD
    row = lax.broadcasted_iota(jnp.int32, (pad, res.shape[1]), 0)
    ones_row = jnp.where(row == 0, 1.0, 0.0).astype(o_ref.dtype)
    for hh in range(res.shape[0] // HEAD):
        o_ref[hh * V_ROWS:hh * V_ROWS + HEAD, :] = res[hh * HEAD:(hh + 1) * HEAD].astype(o_ref.dtype)
        o_ref[hh * V_ROWS + HEAD:(hh + 1) * V_ROWS, :] = ones_row


def _proj_t(h, wt):
    t, d = h.shape
    n = wt.shape[0]
    n_out = n // HEAD * V_ROWS
    return pl.pallas_call(
        _proj_t_kernel,
        out_shape=jax.ShapeDtypeStruct((n_out, t), BF16),
        grid=(t // ROW_TILE,),
        in_specs=[pl.BlockSpec((ROW_TILE, d), lambda i: (i, 0)),
                  pl.BlockSpec((n, d), lambda i: (0, 0))],
        out_specs=pl.BlockSpec((n_out, ROW_TILE), lambda i: (0, i)),
        compiler_params=_params(("parallel",)),
        name="proj_t",
    )(h, wt)


def _proj_gate_kernel(h_ref, w_ref, b_ref, o_ref):
    z = jnp.dot(h_ref[...], w_ref[...], preferred_element_type=F32) + b_ref[...]
    o_ref[...] = (1.0 / (1.0 + jnp.exp(-z))).astype(o_ref.dtype)


def _proj_gate(h, w, b):
    t, d = h.shape
    n = w.shape[1]
    tn = 1024
    return pl.pallas_call(
        _proj_gate_kernel,
        out_shape=jax.ShapeDtypeStruct((t, n), BF16),
        grid=(n // tn, t // ROW_TILE),
        in_specs=[pl.BlockSpec((ROW_TILE, d), lambda j, i: (i, 0)),
                  pl.BlockSpec((d, tn), lambda j, i: (0, j)),
                  pl.BlockSpec((1, tn), lambda j, i: (0, j))],
        out_specs=pl.BlockSpec((ROW_TILE, tn), lambda j, i: (i, j)),
        compiler_params=_params(("parallel", "parallel")),
        name="proj_gate",
    )(h, w, b.reshape(1, n))


def _attn_kernel(*refs, differential, lam_init):
    if differential:
        (q_ref, k_ref, vt_ref, lq1_ref, lk1_ref, lq2_ref, lk2_ref, sub_ref, o_ref,
         q_sc, st_sc, mc_sc, m_sc, acc_sc) = refs
        q = q_ref[...]
        lane = lax.broadcasted_iota(jnp.int32, q.shape, 1)
        zero = jnp.zeros_like(q)
        qs = [jnp.where(lane < B_QK_DIM, q, zero), jnp.where(lane >= B_QK_DIM, q, zero)]
    else:
        q_ref, k_ref, vt_ref, o_ref, q_sc, st_sc, mc_sc, m_sc, acc_sc = refs
        qs = [q_ref[:, g * HEAD:(g + 1) * HEAD] for g in range(q_ref.shape[1] // HEAD)]
    tq = q_ref.shape[0]
    n_chunks = k_ref.shape[0] // KV_CHUNK

    for g, qg in enumerate(qs):
        q_sc[:, g * tq:(g + 1) * tq] = qg.astype(F32).T.astype(BF16)
    m_sc[...] = jnp.full(m_sc.shape, -jnp.inf, F32)
    acc_sc[...] = jnp.zeros(acc_sc.shape, F32)

    def scores(j, slot):
        off = j * KV_CHUNK if isinstance(j, int) else pl.multiple_of(j * KV_CHUNK, KV_CHUNK)
        st = jnp.dot(k_ref[pl.ds(off, KV_CHUNK), :], q_sc[...], preferred_element_type=F32)
        st_sc[slot] = st
        mc_sc[slot] = jnp.max(st, axis=0, keepdims=True)

    def consume(j, slot):
        off = j * KV_CHUNK if isinstance(j, int) else pl.multiple_of(j * KV_CHUNK, KV_CHUNK)
        st = st_sc[slot]
        m_old = m_sc[...]
        m_new = jnp.maximum(m_old, mc_sc[slot])
        alpha = jnp.exp2(m_old - m_new)
        p = jnp.exp2((st - m_new).astype(BF16))
        acc_sc[...] = alpha * acc_sc[...] + jnp.dot(vt_ref[:, pl.ds(off, KV_CHUNK)], p,
                                                    preferred_element_type=F32)
        m_sc[...] = m_new

    def step(j, u):
        scores(j + 1, (u + 1) % SCORE_SLOTS)
        consume(j, u % SCORE_SLOTS)

    def trip(i, c):
        for u in range(SCORE_SLOTS):
            step(i * SCORE_SLOTS + u, u)
        return c

    n_trips = (n_chunks - 1) // SCORE_SLOTS
    scores(0, 0)
    lax.fori_loop(0, n_trips, trip, 0)
    for j in range(n_trips * SCORE_SLOTS, n_chunks - 1):
        step(j, j)
    consume(n_chunks - 1, (n_chunks - 1) % SCORE_SLOTS)

    out = acc_sc[:HEAD, :] / acc_sc[HEAD:HEAD + 1, :]
    if differential:
        lam = (jnp.exp(jnp.sum(lq1_ref[...] * lk1_ref[...], axis=1, keepdims=True))
               - jnp.exp(jnp.sum(lq2_ref[...] * lk2_ref[...], axis=1, keepdims=True)) + lam_init)
        ot = out[:, :tq] - lam * out[:, tq:]
        ms = jnp.mean(ot * ot, axis=0, keepdims=True)
        ot = ot * lax.rsqrt(ms + EPS) * sub_ref[...] * (1.0 - lam_init)
        o_ref[...] = ot.T.astype(o_ref.dtype)
    else:
        for g in range(len(qs)):
            o_ref[:, g * HEAD:(g + 1) * HEAD] = out[:, g * tq:(g + 1) * tq].T.astype(o_ref.dtype)


def _attention(q, k, vt, extra, *, batch, seq, kv_heads, q_cols, differential, lam_init):
    t = q.shape[0]
    assert seq % KV_CHUNK == 0
    n_maps = 2 if differential else q_cols // HEAD
    tq = SCORE_LANES // n_maps
    nq = seq // tq
    in_specs = [pl.BlockSpec((tq, q_cols), lambda b, h, i: (b * nq + i, h)),
                pl.BlockSpec((seq, HEAD), lambda b, h, i: (b, h)),
                pl.BlockSpec((V_ROWS, seq), lambda b, h, i: (h, b))]
    in_specs += [pl.BlockSpec(e.shape, lambda b, h, i: (0, 0)) for e in extra]
    return pl.pallas_call(
        functools.partial(_attn_kernel, differential=differential, lam_init=lam_init),
        out_shape=jax.ShapeDtypeStruct((t, kv_heads * q_cols), BF16),
        grid=(batch, kv_heads, nq),
        in_specs=in_specs,
        out_specs=pl.BlockSpec((tq, q_cols), lambda b, h, i: (b * nq + i, h)),
        scratch_shapes=[pltpu.VMEM((HEAD, SCORE_LANES), BF16),
                        pltpu.VMEM((SCORE_SLOTS, KV_CHUNK, SCORE_LANES), F32),
                        pltpu.VMEM((SCORE_SLOTS, 1, SCORE_LANES), F32),
                        pltpu.VMEM((1, SCORE_LANES), F32),
                        pltpu.VMEM((V_ROWS, SCORE_LANES), F32)],
        compiler_params=_params(("parallel", "parallel", "parallel")),
        name="diff_attention" if differential else "axial_gqa",
    )(q, k, vt, *extra)


def _merge_kernel(oa_ref, ob_ref, ga_ref, gb_ref, x_ref, wba_ref, wbb_ref, wout_ref, gffn_ref, wr_ref,
                  x1_ref, h2_ref, re_ref, rw_ref):
    pa = jnp.dot(oa_ref[...], wba_ref[...], preferred_element_type=F32)
    pb = jnp.dot(ob_ref[...], wbb_ref[...], preferred_element_type=F32)
    merged = ga_ref[...].astype(F32) * pa + gb_ref[...].astype(F32) * pb
    x1 = x_ref[...] + jnp.dot(merged.astype(BF16), wout_ref[...], preferred_element_type=F32)
    x1_ref[...] = x1
    ms = jnp.mean(x1 * x1, axis=-1, keepdims=True)
    h2 = x1 * lax.rsqrt(ms + EPS) * gffn_ref[...]
    _slab_store(h2_ref, h2)

    logits = jnp.dot(h2.astype(BF16), wr_ref[...], preferred_element_type=F32)
    lane = lax.broadcasted_iota(jnp.int32, logits.shape, 1)
    lane_f = lane.astype(F32)
    neg = jnp.float32(-jnp.inf)
    big = jnp.float32(ROUTER_LANES)

    def first_argmax(vals):
        top = jnp.max(vals, axis=1, keepdims=True)
        idx = jnp.min(jnp.where(vals == top, lane_f, big), axis=1, keepdims=True)
        return top, idx

    g_mask = lane < N_GROUPS
    g_logits = jnp.where(g_mask, logits, neg)
    g_top, g_idx = first_argmax(g_logits)
    g_w = 1.0 / jnp.sum(jnp.where(g_mask, jnp.exp(logits - g_top), 0.0), axis=1, keepdims=True)

    e_lo = g_idx * EXPERTS_PER_GROUP + N_GROUPS
    e_mask = (lane_f >= e_lo) & (lane_f < e_lo + EXPERTS_PER_GROUP)
    e_logits = jnp.where(e_mask, logits, neg)
    v1, i1 = first_argmax(e_logits)
    v2, i2 = first_argmax(jnp.where(lane_f == i1, neg, e_logits))
    d = jnp.exp(v2 - v1)
    w1 = g_w / (1.0 + d)
    w2 = g_w * d / (1.0 + d)
    e1 = (i1 - N_GROUPS).astype(jnp.int32)
    e2 = (i2 - N_GROUPS).astype(jnp.int32)
    re_ref[...] = jnp.where(lane == 0, e1, jnp.where(lane == 1, e2, 0))
    rw_ref[...] = jnp.where(lane == 0, w1, jnp.where(lane == 1, w2, 0.0))


def _merge(oa, ob, gates, x, wba, wbb, wout, gffn, wr):
    t, d = x.shape
    tm = MERGE_TILE
    row = lambda i: (i, 0)
    const = lambda i: (0, 0)
    single = pl.Buffered(1)
    return pl.pallas_call(
        _merge_kernel,
        out_shape=(jax.ShapeDtypeStruct((t, d), F32), jax.ShapeDtypeStruct((t * _slab_pitch(d), LANES), F32),
                   jax.ShapeDtypeStruct((t, ROUTER_LANES), jnp.int32),
                   jax.ShapeDtypeStruct((t, ROUTER_LANES), F32)),
        grid=(t // tm,),
        in_specs=[pl.BlockSpec((tm, oa.shape[1]), row),
                  pl.BlockSpec((tm, ob.shape[1]), row),
                  pl.BlockSpec((tm, d), lambda i: (i, 0)),
                  pl.BlockSpec((tm, d), lambda i: (i, 1)),
                  pl.BlockSpec((tm, d), row),
                  pl.BlockSpec(wba.shape, const, pipeline_mode=single),
                  pl.BlockSpec(wbb.shape, const, pipeline_mode=single),
                  pl.BlockSpec(wout.shape, const, pipeline_mode=single),
                  pl.BlockSpec((1, d), const),
                  pl.BlockSpec(wr.shape, const, pipeline_mode=single)],
        out_specs=(pl.BlockSpec((tm, d), row), pl.BlockSpec((tm * _slab_pitch(d), LANES), row),
                   pl.BlockSpec((tm, ROUTER_LANES), row), pl.BlockSpec((tm, ROUTER_LANES), row)),
        compiler_params=_params(("parallel",)),
        name="merge_router",
    )(oa, ob, gates, gates, x, wba, wbb, wout, gffn.reshape(1, d), wr)


def _moe_kernel(be_ref, used_ref, src_ref, src_next_ref, dst_ref, h_hbm, w1_ref, w3_ref, w2_ref, y_hbm,
                xbuf, obuf, gsem, ssem, *, d):
    del be_ref
    i = pl.program_id(0)
    n_used = used_ref[0]
    data = d // LANES
    pitch = _slab_pitch(d)
    cur = i % 2

    def gather_issue(tok_ref, buf):
        def body(r0, c):
            for u in range(DMA_UNROLL):
                r = r0 * DMA_UNROLL + u
                pltpu.make_async_copy(h_hbm.at[pl.ds(tok_ref[0, 0, r] * pitch, data), :],
                                      xbuf.at[buf, pl.ds(r * pitch, data), :], gsem.at[buf]).start()
            return c
        lax.fori_loop(0, MOE_BLOCK // DMA_UNROLL, body, 0)

    def scatter_issue():
        def body(r0, c):
            for u in range(DMA_UNROLL):
                r = r0 * DMA_UNROLL + u
                pltpu.make_async_copy(obuf.at[pl.ds(r * pitch, pitch), :],
                                      y_hbm.at[pl.ds(dst_ref[0, 0, r] * pitch, pitch), :], ssem).start()
            return c
        lax.fori_loop(0, MOE_BLOCK // DMA_UNROLL, body, 0)

    def gather_wait(buf):
        n = MOE_BLOCK * data
        pltpu.make_async_copy(h_hbm.at[pl.ds(0, n), :], xbuf.at[buf, pl.ds(0, n), :], gsem.at[buf]).wait()

    def scatter_wait():
        pltpu.make_async_copy(obuf, y_hbm.at[pl.ds(0, MOE_BLOCK * pitch), :], ssem).wait()

    @pl.when(i == 0)
    def _():
        gather_issue(src_ref, 0)
        obuf[...] = jnp.zeros(obuf.shape, F32)
        n_rows = y_hbm.shape[0]
        init = pltpu.make_async_copy(obuf, y_hbm.at[pl.ds(n_rows - MOE_BLOCK * pitch, MOE_BLOCK * pitch), :], ssem)
        init.start()
        init.wait()

    @pl.when(i < n_used)
    def _():
        gather_wait(cur)

        @pl.when(i + 1 < n_used)
        def _():
            gather_issue(src_next_ref, 1 - cur)

        xb = _slab_load(xbuf.at[cur], MOE_BLOCK, d).astype(BF16)
        a1 = jnp.dot(xb, w1_ref[...], preferred_element_type=F32)
        a3 = jnp.dot(xb, w3_ref[...], preferred_element_type=F32)
        u = a1 * (1.0 / (1.0 + jnp.exp(-a1))) * a3
        out = jnp.dot(u.astype(BF16), w2_ref[...], preferred_element_type=F32)

        @pl.when(i > 0)
        def _():
            scatter_wait()

        _slab_store(obuf, out)
        scatter_issue()

        @pl.when(i == n_used - 1)
        def _():
            scatter_wait()


def _moe(block_expert, n_used, src_tok, dst_row, h2_slab, w1, w3, w2, *, n_out_rows):
    n_blocks = src_tok.shape[0]
    d, ff = w1.shape[1], w1.shape[2]
    slab = _slab_pitch(d)
    grid_spec = pltpu.PrefetchScalarGridSpec(
        num_scalar_prefetch=2,
        grid=(n_blocks,),
        in_specs=[pl.BlockSpec((1, 1, MOE_BLOCK), lambda i, be, nu: (i, 0, 0), memory_space=pltpu.SMEM),
                  pl.BlockSpec((1, 1, MOE_BLOCK), lambda i, be, nu: (jnp.minimum(i + 1, n_blocks - 1), 0, 0),
                               memory_space=pltpu.SMEM),
                  pl.BlockSpec((1, 1, MOE_BLOCK), lambda i, be, nu: (i, 0, 0), memory_space=pltpu.SMEM),
                  pl.BlockSpec(memory_space=pl.ANY),
                  pl.BlockSpec((None, d, ff), lambda i, be, nu: (be[i], 0, 0)),
                  pl.BlockSpec((None, d, ff), lambda i, be, nu: (be[i], 0, 0)),
                  pl.BlockSpec((None, ff, d), lambda i, be, nu: (be[i], 0, 0))],
        out_specs=pl.BlockSpec(memory_space=pl.ANY),
        scratch_shapes=[pltpu.VMEM((2, MOE_BLOCK * slab, LANES), F32), pltpu.VMEM((MOE_BLOCK * slab, LANES), F32),
                        pltpu.SemaphoreType.DMA((2,)), pltpu.SemaphoreType.DMA],
    )
    return pl.pallas_call(
        functools.partial(_moe_kernel, d=d),
        out_shape=jax.ShapeDtypeStruct((n_out_rows * slab, LANES), F32),
        grid_spec=grid_spec,
        compiler_params=_params(("arbitrary",)),
        name="expert_mlp",
    )(block_expert, n_used, src_tok, src_tok, dst_row, h2_slab, w1, w3, w2)


def _routing_plan(route_e, n_tokens):
    n_assign = n_tokens * TOP_K
    e_flat = route_e.reshape(n_assign)
    order = jnp.argsort(e_flat).astype(jnp.int32)
    counts = jnp.zeros((N_EXPERTS,), jnp.int32).at[e_flat].add(1)
    offsets = jnp.cumsum(counts) - counts
    blocks_per_expert = (counts + MOE_BLOCK - 1) // MOE_BLOCK
    block_ends = jnp.cumsum(blocks_per_expert)
    n_blocks = n_assign // MOE_BLOCK + N_EXPERTS
    block_id = jnp.arange(n_blocks, dtype=jnp.int32)
    block_expert = jnp.clip(jnp.searchsorted(block_ends, block_id, side="right"), 0, N_EXPERTS - 1).astype(jnp.int32)
    first_block = (block_ends - blocks_per_expert)[block_expert]
    in_block = jnp.arange(MOE_BLOCK, dtype=jnp.int32)[None, :]
    rank = (block_id - first_block)[:, None] * MOE_BLOCK + in_block
    valid = (rank < counts[block_expert][:, None]) & (block_id < block_ends[-1])[:, None]
    assign = order[jnp.clip(offsets[block_expert][:, None] + rank, 0, n_assign - 1)]
    token = assign // TOP_K
    src_tok = jnp.where(valid, token, 0).astype(jnp.int32)
    dst_row = jnp.where(valid, (assign % TOP_K) * n_tokens + token, n_assign + in_block).astype(jnp.int32)
    shape = (n_blocks, 1, MOE_BLOCK)
    return block_expert, block_ends[-1:].astype(jnp.int32), src_tok.reshape(shape), dst_row.reshape(shape)


def _final_kernel(x1_ref, y0_ref, y1_ref, rw_ref, g_ref, o_ref, *, normalize):
    rw = rw_ref[...]
    lane = lax.broadcasted_iota(jnp.int32, rw.shape, 1)
    w1 = jnp.sum(jnp.where(lane == 0, rw, 0.0), axis=1, keepdims=True)
    w2 = jnp.sum(jnp.where(lane == 1, rw, 0.0), axis=1, keepdims=True)
    rows, d = x1_ref.shape
    x2 = x1_ref[...] + (_slab_load(y0_ref, rows, d) * w1 + _slab_load(y1_ref, rows, d) * w2)
    if normalize:
        ms = jnp.mean(x2 * x2, axis=-1, keepdims=True)
        x2 = x2 * lax.rsqrt(ms + EPS) * g_ref[...]
    o_ref[...] = x2


def _final(x1, y2, route_w, g, *, normalize):
    t, d = x1.shape
    tm = MERGE_TILE
    nt = t // tm
    return pl.pallas_call(
        functools.partial(_final_kernel, normalize=normalize),
        out_shape=jax.ShapeDtypeStruct((t, d), F32),
        grid=(nt,),
        in_specs=[pl.BlockSpec((tm, d), lambda i: (i, 0)),
                  pl.BlockSpec((tm * _slab_pitch(d), LANES), lambda i: (i, 0)),
                  pl.BlockSpec((tm * _slab_pitch(d), LANES), lambda i: (i + nt, 0)),
                  pl.BlockSpec((tm, ROUTER_LANES), lambda i: (i, 0)),
                  pl.BlockSpec((1, d), lambda i: (0, 0))],
        out_specs=pl.BlockSpec((tm, d), lambda i: (i, 0)),
        compiler_params=_params(("parallel",)),
        name="combine_norm",
    )(x1, y2, y2, route_w, g.reshape(1, d))


def _rope_tables(seq):
    pos = jnp.arange(seq, dtype=jnp.int32)
    inv = ROPE_THETA ** (-jnp.arange(0, 2 * ROPE_HALF, 2, dtype=F32) / (2 * ROPE_HALF))

    def cs(p):
        ang = p.astype(F32)[:, None] * inv[None, :]
        return jnp.cos(ang), jnp.sin(ang)

    cr, sr = cs(pos // GRID_W)
    cc, sc = cs(pos % GRID_W)
    cp, sp = cs(pos)
    axial = (jnp.concatenate([cr, cr, cc, cc], axis=1), jnp.concatenate([-sr, sr, -sc, sc], axis=1))
    linear = (jnp.concatenate([cp, cp, cp, cp], axis=1), jnp.concatenate([-sp, sp, -sp, sp], axis=1))
    return axial, linear


def kernel(x, g_mix, w_in, q_norm_a, k_norm_a, lam_q1, lam_k1, lam_q2, lam_k2, subln_b, w_branch_a,
           w_branch_b, w_gate, b_gate, w_out, g_ffn, w_router_group, w_router_expert, w_e_gate, w_e_up,
           w_e_down, g_final):
    batch, seq, d = x.shape
    t = batch * seq
    depth = g_mix.shape[0]
    a_q = A_HEADS * HEAD
    a_kv = A_KV_HEADS * HEAD
    b_w = B_HEADS * HEAD
    cuts = [0, a_q, a_q + a_kv, a_q + 2 * a_kv, a_q + 2 * a_kv + b_w, a_q + 2 * a_kv + 2 * b_w,
            a_q + 2 * a_kv + 3 * b_w]
    (cos_a, sin_a), (cos_b, sin_b) = _rope_tables(seq)
    scale_a = HEAD ** -0.5 * math.log2(math.e)
    scale_b = B_QK_DIM ** -0.5 * math.log2(math.e)
    ones = jnp.ones((HEAD,), F32)

    xf = x.reshape(t, d)
    for l in range(depth):
        lam_init = 0.8 - 0.6 * math.exp(-0.3 * l)
        w = w_in[l].astype(BF16)
        w_qa, w_ka, w_va, w_qb, w_kb, w_vb = (w[:, cuts[i]:cuts[i + 1]] for i in range(6))

        h = _rmsnorm(xf, g_mix[l], BF16)
        qa =_proj_rope(h, w_qa, cos_a * scale_a, sin_a * scale_a, q_norm_a[l], normalize=True, seq=seq)
        ka = _proj_rope(h, w_ka, cos_a, sin_a, k_norm_a[l], normalize=True, seq=seq)
        vat = _proj_t(h, w_va.T)
        qb = _proj_rope(h, w_qb, cos_b * scale_b, sin_b * scale_b, ones, normalize=False, seq=seq)
        kb = _proj_rope(h, w_kb, cos_b, sin_b, ones, normalize=False, seq=seq)
        vbt = _proj_t(h, w_vb.T)
        gates = _proj_gate(h, w_gate[l].astype(BF16), b_gate[l])

        o_a = _attention(qa, ka, vat, [], batch=batch, seq=seq, kv_heads=A_KV_HEADS, q_cols=A_GROUP * HEAD,
                         differential=False, lam_init=lam_init)
        lam_rows = [v[l].reshape(1, B_QK_DIM) for v in (lam_q1, lam_k1, lam_q2, lam_k2)]
        o_b = _attention(qb, kb, vbt, lam_rows + [subln_b[l].reshape(HEAD, 1)], batch=batch, seq=seq,
                         kv_heads=B_HEADS, q_cols=HEAD, differential=True, lam_init=lam_init)

        w_r = jnp.zeros((d, ROUTER_LANES), F32)
        w_r = w_r.at[:, :N_GROUPS].set(w_router_group[l]).at[:, N_GROUPS:N_GROUPS + N_EXPERTS].set(w_router_expert[l])
        x1, h2, route_e, route_w = _merge(o_a, o_b, gates, xf, w_branch_a[l].astype(BF16),
                                          w_branch_b[l].astype(BF16), w_out[l].astype(BF16), g_ffn[l],
                                          w_r.astype(BF16))

        block_expert, n_used, src_tok, dst_row = _routing_plan(route_e[:, :TOP_K], t)
        y2 = _moe(block_expert, n_used, src_tok, dst_row, h2, w_e_gate[l].astype(BF16), w_e_up[l].astype(BF16),
                  w_e_down[l].astype(BF16), n_out_rows=TOP_K * t + MOE_BLOCK)
        xf = _final(x1, y2, route_w, g_final, normalize=(l + 1 == depth))
    return xf.reshape(batch, seq, d)
```

```python
import functools
import math

import jax
import jax.numpy as jnp
from jax import lax
from jax.experimental import pallas as pl
from jax.experimental.pallas import tpu as pltpu

F32 = jnp.float32
BF16 = jnp.bfloat16

GRID_W = 64
ROPE_THETA = 10000.0
EPS = 1e-6
LANES = 128
HEAD = 128
V_ROWS = HEAD + 16
ROPE_HALF = 32
A_HEADS = 8
A_KV_HEADS = 2
A_GROUP = A_HEADS // A_KV_HEADS
B_HEADS = 8
B_QK_DIM = 64
N_GROUPS = 4
EXPERTS_PER_GROUP = 8
N_EXPERTS = N_GROUPS * EXPERTS_PER_GROUP
TOP_K = 2
ROUTER_LANES = 128

VMEM_LIMIT = 56 * 1024 * 1024

ROW_TILE = 512
MERGE_TILE = 256
SCORE_LANES = 1024
KV_CHUNK = 512
SCORE_SLOTS = 3
MOE_BLOCK = 256
DMA_UNROLL = 8


def _params(sem):
    return pltpu.CompilerParams(dimension_semantics=sem, vmem_limit_bytes=VMEM_LIMIT)


def _slab_pitch(d):
    return d // LANES + 1


def _slab_store(ref, val):
    rows, d = val.shape
    pitch = _slab_pitch(d)
    for s in range(d // LANES):
        ref[pl.ds(s, rows, stride=pitch), :] = val[:, s * LANES:(s + 1) * LANES]
    ref[pl.ds(d // LANES, rows, stride=pitch), :] = jnp.zeros((rows, LANES), val.dtype)


def _slab_load(ref, rows, d):
    pitch = _slab_pitch(d)
    return jnp.concatenate([ref[pl.ds(s, rows, stride=pitch), :] for s in range(d // LANES)], axis=1)


def _rmsnorm_kernel(x_ref, g_ref, o_ref):
    x = x_ref[...]
    ms = jnp.mean(x * x, axis=-1, keepdims=True)
    o_ref[...] = (x * lax.rsqrt(ms + EPS) * g_ref[...]).astype(o_ref.dtype)


def _rmsnorm(x, g, out_dtype):
    t, d = x.shape
    return pl.pallas_call(
        _rmsnorm_kernel,
        out_shape=jax.ShapeDtypeStruct((t, d), out_dtype),
        grid=(t // ROW_TILE,),
        in_specs=[pl.BlockSpec((ROW_TILE, d), lambda i: (i, 0)),
                  pl.BlockSpec((1, d), lambda i: (0, 0))],
        out_specs=pl.BlockSpec((ROW_TILE, d), lambda i: (i, 0)),
        compiler_params=_params(("parallel",)),
        name="rmsnorm",
    )(x, g.reshape(1, d))


def _rotary(t, cos, sin):
    lane = lax.broadcasted_iota(jnp.int32, t.shape, 1)
    first = (lane % (2 * ROPE_HALF)) < ROPE_HALF
    partner = jnp.where(first, pltpu.roll(t, HEAD - ROPE_HALF, 1), pltpu.roll(t, ROPE_HALF, 1))
    return t * cos + partner * sin


def _proj_rope_kernel(h_ref, w_ref, cos_ref, sin_ref, g_ref, o_ref, *, normalize):
    acc = jnp.dot(h_ref[...], w_ref[...], preferred_element_type=F32)
    cos = cos_ref[...]
    sin = sin_ref[...]
    for c in range(acc.shape[1] // HEAD):
        t = acc[:, c * HEAD:(c + 1) * HEAD]
        if normalize:
            ms = jnp.mean(t * t, axis=-1, keepdims=True)
            t = t * lax.rsqrt(ms + EPS) * g_ref[...]
        o_ref[:, c * HEAD:(c + 1) * HEAD] = _rotary(t, cos, sin).astype(o_ref.dtype)


def _proj_rope(h, w, cos, sin, gain, *, normalize, seq):
    t, d = h.shape
    n = w.shape[1]
    tiles_per_seq = seq // ROW_TILE
    return pl.pallas_call(
        functools.partial(_proj_rope_kernel, normalize=normalize),
        out_shape=jax.ShapeDtypeStruct((t, n), BF16),
        grid=(t // ROW_TILE,),
        in_specs=[pl.BlockSpec((ROW_TILE, d), lambda i: (i, 0)),
                  pl.BlockSpec((d, n), lambda i: (0, 0)),
                  pl.BlockSpec((ROW_TILE, HEAD), lambda i: (i % tiles_per_seq, 0)),
                  pl.BlockSpec((ROW_TILE, HEAD), lambda i: (i % tiles_per_seq, 0)),
                  pl.BlockSpec((1, HEAD), lambda i: (0, 0))],
        out_specs=pl.BlockSpec((ROW_TILE, n), lambda i: (i, 0)),
        compiler_params=_params(("parallel",)),
        name="proj_rope",
    )(h, w, cos, sin, gain.reshape(1, HEAD))


def _proj_t_kernel(h_ref, wt_ref, o_ref):
    res = lax.dot_general(wt_ref[...], h_ref[...], (((1,), (1,)), ((), ())), preferred_element_type=F32)
    pad = V_ROWS - HEAD
    row = lax.broadcasted_iota(jnp.int32, (pad, res.shape[1]), 0)
    ones_row = jnp.where(row == 0, 1.0, 0.0).astype(o_ref.dtype)
    for hh in range(res.shape[0] // HEAD):
        o_ref[hh * V_ROWS:hh * V_ROWS + HEAD, :] = res[hh * HEAD:(hh + 1) * HEAD].astype(o_ref.dtype)
        o_ref[hh * V_ROWS + HEAD:(hh + 1) * V_ROWS, :] = ones_row


def _proj_t(h, wt):
    t, d = h.shape
    n = wt.shape[0]
    n_out = n // HEAD * V_ROWS
    return pl.pallas_call(
        _proj_t_kernel,
        out_shape=jax.ShapeDtypeStruct((n_out, t), BF16),
        grid=(t // ROW_TILE,),
        in_specs=[pl.BlockSpec((ROW_TILE, d), lambda i: (i, 0)),
                  pl.BlockSpec((n, d), lambda i: (0, 0))],
        out_specs=pl.BlockSpec((n_out, ROW_TILE), lambda i: (0, i)),
        compiler_params=_params(("parallel",)),
        name="proj_t",
    )(h, wt)


def _proj_gate_kernel(h_ref, w_ref, b_ref, o_ref):
    z = jnp.dot(h_ref[...], w_ref[...], preferred_element_type=F32) + b_ref[...]
    o_ref[...] = (1.0 / (1.0 + jnp.exp(-z))).astype(o_ref.dtype)


def _proj_gate(h, w, b):
    t, d = h.shape
    n = w.shape[1]
    tn = 1024
    return pl.pallas_call(
        _proj_gate_kernel,
        out_shape=jax.ShapeDtypeStruct((t, n), BF16),
        grid=(n // tn, t // ROW_TILE),
        in_specs=[pl.BlockSpec((ROW_TILE, d), lambda j, i: (i, 0)),
                  pl.BlockSpec((d, tn), lambda j, i: (0, j)),
                  pl.BlockSpec((1, tn), lambda j, i: (0, j))],
        out_specs=pl.BlockSpec((ROW_TILE, tn), lambda j, i: (i, j)),
        compiler_params=_params(("parallel", "parallel")),
        name="proj_gate",
    )(h, w, b.reshape(1, n))


def _attn_kernel(*refs, differential, lam_init):
    if differential:
        (q_ref, k_ref, vt_ref, lq1_ref, lk1_ref, lq2_ref, lk2_ref, sub_ref, o_ref,
         q_sc, st_sc, mc_sc, m_sc, acc_sc) = refs
        q = q_ref[...]
        lane = lax.broadcasted_iota(jnp.int32, q.shape, 1)
        zero = jnp.zeros_like(q)
        qs = [jnp.where(lane < B_QK_DIM, q, zero), jnp.where(lane >= B_QK_DIM, q, zero)]
    else:
        q_ref, k_ref, vt_ref, o_ref, q_sc, st_sc, mc_sc, m_sc, acc_sc = refs
        qs = [q_ref[:, g * HEAD:(g + 1) * HEAD] for g in range(q_ref.shape[1] // HEAD)]
    tq = q_ref.shape[0]
    n_chunks = k_ref.shape[0] // KV_CHUNK

    for g, qg in enumerate(qs):
        q_sc[:, g * tq:(g + 1) * tq] = qg.astype(F32).T.astype(BF16)
    m_sc[...] = jnp.full(m_sc.shape, -jnp.inf, F32)
    acc_sc[...] = jnp.zeros(acc_sc.shape, F32)

    def scores(j, slot):
        off = j * KV_CHUNK if isinstance(j, int) else pl.multiple_of(j * KV_CHUNK, KV_CHUNK)
        st = jnp.dot(k_ref[pl.ds(off, KV_CHUNK), :], q_sc[...], preferred_element_type=F32)
        st_sc[slot] = st
        mc_sc[slot] = jnp.max(st, axis=0, keepdims=True)

    def consume(j, slot):
        off = j * KV_CHUNK if isinstance(j, int) else pl.multiple_of(j * KV_CHUNK, KV_CHUNK)
        st = st_sc[slot]
        m_old = m_sc[...]
        m_new = jnp.maximum(m_old, mc_sc[slot])
        alpha = jnp.exp2(m_old - m_new)
        p = jnp.exp2(st - m_new).astype(BF16)
        acc_sc[...] = alpha * acc_sc[...] + jnp.dot(vt_ref[:, pl.ds(off, KV_CHUNK)], p,
                                                    preferred_element_type=F32)
        m_sc[...] = m_new

    def step(j, u):
        scores(j + 1, (u + 1) % SCORE_SLOTS)
        consume(j, u % SCORE_SLOTS)

    def trip(i, c):
        for u in range(SCORE_SLOTS):
            step(i * SCORE_SLOTS + u, u)
        return c

    n_trips = (n_chunks - 1) // SCORE_SLOTS
    scores(0, 0)
    lax.fori_loop(0, n_trips, trip, 0)
    for j in range(n_trips * SCORE_SLOTS, n_chunks - 1):
        step(j, j)
    consume(n_chunks - 1, (n_chunks - 1) % SCORE_SLOTS)

    out = acc_sc[:HEAD, :] / acc_sc[HEAD:HEAD + 1, :]
    if differential:
        lam = (jnp.exp(jnp.sum(lq1_ref[...] * lk1_ref[...], axis=1, keepdims=True))
               - jnp.exp(jnp.sum(lq2_ref[...] * lk2_ref[...], axis=1, keepdims=True)) + lam_init)
        ot = out[:, :tq] - lam * out[:, tq:]
        ms = jnp.mean(ot * ot, axis=0, keepdims=True)
        ot = ot * lax.rsqrt(ms + EPS) * sub_ref[...] * (1.0 - lam_init)
        o_ref[...] = ot.T.astype(o_ref.dtype)
    else:
        for g in range(len(qs)):
            o_ref[:, g * HEAD:(g + 1) * HEAD] = out[:, g * tq:(g + 1) * tq].T.astype(o_ref.dtype)


def _attention(q, k, vt, extra, *, batch, seq, kv_heads, q_cols, differential, lam_init):
    t = q.shape[0]
    assert seq % KV_CHUNK == 0
    n_maps = 2 if differential else q_cols // HEAD
    tq = SCORE_LANES // n_maps
    nq = seq // tq
    in_specs = [pl.BlockSpec((tq, q_cols), lambda b, h, i: (b * nq + i, h)),
                pl.BlockSpec((seq, HEAD), lambda b, h, i: (b, h)),
                pl.BlockSpec((V_ROWS, seq), lambda b, h, i: (h, b))]
    in_specs += [pl.BlockSpec(e.shape, lambda b, h, i: (0, 0)) for e in extra]
    return pl.pallas_call(
        functools.partial(_attn_kernel, differential=differential, lam_init=lam_init),
        out_shape=jax.ShapeDtypeStruct((t, kv_heads * q_cols), BF16),
        grid=(batch, kv_heads, nq),
        in_specs=in_specs,
        out_specs=pl.BlockSpec((tq, q_cols), lambda b, h, i: (b * nq + i, h)),
        scratch_shapes=[pltpu.VMEM((HEAD, SCORE_LANES), BF16),
                        pltpu.VMEM((SCORE_SLOTS, KV_CHUNK, SCORE_LANES), F32),
                        pltpu.VMEM((SCORE_SLOTS, 1, SCORE_LANES), F32),
                        pltpu.VMEM((1, SCORE_LANES), F32),
                        pltpu.VMEM((V_ROWS, SCORE_LANES), F32)],
        compiler_params=_params(("parallel", "parallel", "parallel")),
        name="diff_attention" if differential else "axial_gqa",
    )(q, k, vt, *extra)


def _merge_kernel(oa_ref, ob_ref, ga_ref, gb_ref, x_ref, wba_ref, wbb_ref, wout_ref, gffn_ref, wr_ref,
                  x1_ref, h2_ref, re_ref, rw_ref):
    pa = jnp.dot(oa_ref[...], wba_ref[...], preferred_element_type=F32)
    pb = jnp.dot(ob_ref[...], wbb_ref[...], preferred_element_type=F32)
    merged = ga_ref[...].astype(F32) * pa + gb_ref[...].astype(F32) * pb
    x1 = x_ref[...] + jnp.dot(merged.astype(BF16), wout_ref[...], preferred_element_type=F32)
    x1_ref[...] = x1
    ms = jnp.mean(x1 * x1, axis=-1, keepdims=True)
    h2 = x1 * lax.rsqrt(ms + EPS) * gffn_ref[...]
    _slab_store(h2_ref, h2)

    logits = jnp.dot(h2.astype(BF16), wr_ref[...], preferred_element_type=F32)
    lane = lax.broadcasted_iota(jnp.int32, logits.shape, 1)
    lane_f = lane.astype(F32)
    neg = jnp.float32(-jnp.inf)
    big = jnp.float32(ROUTER_LANES)

    def first_argmax(vals):
        top = jnp.max(vals, axis=1, keepdims=True)
        idx = jnp.min(jnp.where(vals == top, lane_f, big), axis=1, keepdims=True)
        return top, idx

    g_mask = lane < N_GROUPS
    g_logits = jnp.where(g_mask, logits, neg)
    g_top, g_idx = first_argmax(g_logits)
    g_w = 1.0 / jnp.sum(jnp.where(g_mask, jnp.exp(logits - g_top), 0.0), axis=1, keepdims=True)

    e_lo = g_idx * EXPERTS_PER_GROUP + N_GROUPS
    e_mask = (lane_f >= e_lo) & (lane_f < e_lo + EXPERTS_PER_GROUP)
    e_logits = jnp.where(e_mask, logits, neg)
    v1, i1 = first_argmax(e_logits)
    v2, i2 = first_argmax(jnp.where(lane_f == i1, neg, e_logits))
    d = jnp.exp(v2 - v1)
    w1 = g_w / (1.0 + d)
    w2 = g_w * d / (1.0 + d)
    e1 = (i1 - N_GROUPS).astype(jnp.int32)
    e2 = (i2 - N_GROUPS).astype(jnp.int32)
    re_ref[...] = jnp.where(lane == 0, e1, jnp.where(lane == 1, e2, 0))
    rw_ref[...] = jnp.where(lane == 0, w1, jnp.where(lane == 1, w2, 0.0))


def _merge(oa, ob, gates, x, wba, wbb, wout, gffn, wr):
    t, d = x.shape
    tm = MERGE_TILE
    row = lambda i: (i, 0)
    const = lambda i: (0, 0)
    single = pl.Buffered(1)
    return pl.pallas_call(
        _merge_kernel,
        out_shape=(jax.ShapeDtypeStruct((t, d), F32), jax.ShapeDtypeStruct((t * _slab_pitch(d), LANES), F32),
                   jax.ShapeDtypeStruct((t, ROUTER_LANES), jnp.int32),
                   jax.ShapeDtypeStruct((t, ROUTER_LANES), F32)),
        grid=(t // tm,),
        in_specs=[pl.BlockSpec((tm, oa.shape[1]), row),
                  pl.BlockSpec((tm, ob.shape[1]), row),
                  pl.BlockSpec((tm, d), lambda i: (i, 0)),
                  pl.BlockSpec((tm, d), lambda i: (i, 1)),
                  pl.BlockSpec((tm, d), row),
                  pl.BlockSpec(wba.shape, const, pipeline_mode=single),
                  pl.BlockSpec(wbb.shape, const, pipeline_mode=single),
                  pl.BlockSpec(wout.shape, const, pipeline_mode=single),
                  pl.BlockSpec((1, d), const),
                  pl.BlockSpec(wr.shape, const, pipeline_mode=single)],
        out_specs=(pl.BlockSpec((tm, d), row), pl.BlockSpec((tm * _slab_pitch(d), LANES), row),
                   pl.BlockSpec((tm, ROUTER_LANES), row), pl.BlockSpec((tm, ROUTER_LANES), row)),
        compiler_params=_params(("parallel",)),
        name="merge_router",
    )(oa, ob, gates, gates, x, wba, wbb, wout, gffn.reshape(1, d), wr)


def _moe_kernel(be_ref, used_ref, src_ref, src_next_ref, dst_ref, h_hbm, w1_ref, w3_ref, w2_ref, y_hbm,
                xbuf, obuf, gsem, ssem, *, d):
    del be_ref
    i = pl.program_id(0)
    n_used = used_ref[0]
    data = d // LANES
    pitch = _slab_pitch(d)
    cur = i % 2

    def gather_issue(tok_ref, buf):
        def body(r0, c):
            for u in range(DMA_UNROLL):
                r = r0 * DMA_UNROLL + u
                pltpu.make_async_copy(h_hbm.at[pl.ds(tok_ref[0, 0, r] * pitch, data), :],
                                      xbuf.at[buf, pl.ds(r * pitch, data), :], gsem.at[buf]).start(priority=u % 2)
            return c
        lax.fori_loop(0, MOE_BLOCK // DMA_UNROLL, body, 0)

    def scatter_issue():
        def body(r0, c):
            for u in range(DMA_UNROLL):
                r = r0 * DMA_UNROLL + u
                pltpu.make_async_copy(obuf.at[pl.ds(r * pitch, pitch), :],
                                      y_hbm.at[pl.ds(dst_ref[0, 0, r] * pitch, pitch), :], ssem).start(priority=u % 2)
            return c
        lax.fori_loop(0, MOE_BLOCK // DMA_UNROLL, body, 0)

    def gather_wait(buf):
        n = MOE_BLOCK * data
        pltpu.make_async_copy(h_hbm.at[pl.ds(0, n), :], xbuf.at[buf, pl.ds(0, n), :], gsem.at[buf]).wait()

    def scatter_wait():
        pltpu.make_async_copy(obuf, y_hbm.at[pl.ds(0, MOE_BLOCK * pitch), :], ssem).wait()

    @pl.when(i == 0)
    def _():
        gather_issue(src_ref, 0)
        obuf[...] = jnp.zeros(obuf.shape, F32)
        n_rows = y_hbm.shape[0]
        init = pltpu.make_async_copy(obuf, y_hbm.at[pl.ds(n_rows - MOE_BLOCK * pitch, MOE_BLOCK * pitch), :], ssem)
        init.start()
        init.wait()

    @pl.when(i < n_used)
    def _():
        gather_wait(cur)

        @pl.when(i + 1 < n_used)
        def _():
            gather_issue(src_next_ref, 1 - cur)

        xb = _slab_load(xbuf.at[cur], MOE_BLOCK, d).astype(BF16)
        a1 = jnp.dot(xb, w1_ref[...], preferred_element_type=F32)
        a3 = jnp.dot(xb, w3_ref[...], preferred_element_type=F32)
        u = a1 * (1.0 / (1.0 + jnp.exp(-a1))) * a3
        out = jnp.dot(u.astype(BF16), w2_ref[...], preferred_element_type=F32)

        @pl.when(i > 0)
        def _():
            scatter_wait()

        _slab_store(obuf, out)
        scatter_issue()

        @pl.when(i == n_used - 1)
        def _():
            scatter_wait()


def _moe(block_expert, n_used, src_tok, dst_row, h2_slab, w1, w3, w2, *, n_out_rows):
    n_blocks = src_tok.shape[0]
    d, ff = w1.shape[1], w1.shape[2]
    slab = _slab_pitch(d)
    grid_spec = pltpu.PrefetchScalarGridSpec(
        num_scalar_prefetch=2,
        grid=(n_blocks,),
        in_specs=[pl.BlockSpec((1, 1, MOE_BLOCK), lambda i, be, nu: (i, 0, 0), memory_space=pltpu.SMEM),
                  pl.BlockSpec((1, 1, MOE_BLOCK), lambda i, be, nu: (jnp.minimum(i + 1, n_blocks - 1), 0, 0),
                               memory_space=pltpu.SMEM),
                  pl.BlockSpec((1, 1, MOE_BLOCK), lambda i, be, nu: (i, 0, 0), memory_space=pltpu.SMEM),
                  pl.BlockSpec(memory_space=pl.ANY),
                  pl.BlockSpec((None, d, ff), lambda i, be, nu: (be[i], 0, 0)),
                  pl.BlockSpec((None, d, ff), lambda i, be, nu: (be[i], 0, 0)),
                  pl.BlockSpec((None, ff, d), lambda i, be, nu: (be[i], 0, 0))],
        out_specs=pl.BlockSpec(memory_space=pl.ANY),
        scratch_shapes=[pltpu.VMEM((2, MOE_BLOCK * slab, LANES), F32), pltpu.VMEM((MOE_BLOCK * slab, LANES), F32),
                        pltpu.SemaphoreType.DMA((2,)), pltpu.SemaphoreType.DMA],
    )
    return pl.pallas_call(
        functools.partial(_moe_kernel, d=d),
        out_shape=jax.ShapeDtypeStruct((n_out_rows * slab, LANES), F32),
        grid_spec=grid_spec,
        compiler_params=_params(("arbitrary",)),
        name="expert_mlp",
    )(block_expert, n_used, src_tok, src_tok, dst_row, h2_slab, w1, w3, w2)


def _routing_plan(route_e, n_tokens):
    n_assign = n_tokens * TOP_K
    e_flat = route_e.reshape(n_assign)
    order = jnp.argsort(e_flat).astype(jnp.int32)
    experts = jnp.arange(N_EXPERTS, dtype=jnp.int32)
    counts = jnp.sum((e_flat[:, None] == experts[None, :]).astype(jnp.int32), axis=0)
    offsets = jnp.cumsum(counts) - counts
    blocks_per_expert = (counts + MOE_BLOCK - 1) // MOE_BLOCK
    block_ends = jnp.cumsum(blocks_per_expert)
    n_blocks = n_assign // MOE_BLOCK + N_EXPERTS
    block_id = jnp.arange(n_blocks, dtype=jnp.int32)
    block_expert = jnp.sum((block_ends[None, :] <= block_id[:, None]).astype(jnp.int32), axis=1)
    block_expert = jnp.minimum(block_expert, N_EXPERTS - 1)
    first_block = (block_ends - blocks_per_expert)[block_expert]
    in_block = jnp.arange(MOE_BLOCK, dtype=jnp.int32)[None, :]
    rank = (block_id - first_block)[:, None] * MOE_BLOCK + in_block
    valid = (rank < counts[block_expert][:, None]) & (block_id < block_ends[-1])[:, None]
    assign = order[jnp.clip(offsets[block_expert][:, None] + rank, 0, n_assign - 1)]
    token = assign // TOP_K
    src_tok = jnp.where(valid, token, 0).astype(jnp.int32)
    dst_row = jnp.where(valid, (assign % TOP_K) * n_tokens + token, n_assign + in_block).astype(jnp.int32)
    shape = (n_blocks, 1, MOE_BLOCK)
    return block_expert, block_ends[-1:].astype(jnp.int32), src_tok.reshape(shape), dst_row.reshape(shape)


def _final_kernel(x1_ref, y0_ref, y1_ref, rw_ref, g_ref, o_ref, *, normalize):
    rw = rw_ref[...]
    lane = lax.broadcasted_iota(jnp.int32, rw.shape, 1)
    w1 = jnp.sum(jnp.where(lane == 0, rw, 0.0), axis=1, keepdims=True)
    w2 = jnp.sum(jnp.where(lane == 1, rw, 0.0), axis=1, keepdims=True)
    rows, d = x1_ref.shape
    x2 = x1_ref[...] + (_slab_load(y0_ref, rows, d) * w1 + _slab_load(y1_ref, rows, d) * w2)
    if normalize:
        ms = jnp.mean(x2 * x2, axis=-1, keepdims=True)
        x2 = x2 * lax.rsqrt(ms + EPS) * g_ref[...]
    o_ref[...] = x2


def _final(x1, y2, route_w, g, *, normalize):
    t, d = x1.shape
    tm = MERGE_TILE
    nt = t // tm
    return pl.pallas_call(
        functools.partial(_final_kernel, normalize=normalize),
        out_shape=jax.ShapeDtypeStruct((t, d), F32),
        grid=(nt,),
        in_specs=[pl.BlockSpec((tm, d), lambda i: (i, 0)),
                  pl.BlockSpec((tm * _slab_pitch(d), LANES), lambda i: (i, 0)),
                  pl.BlockSpec((tm * _slab_pitch(d), LANES), lambda i: (i + nt, 0)),
                  pl.BlockSpec((tm, ROUTER_LANES), lambda i: (i, 0)),
                  pl.BlockSpec((1, d), lambda i: (0, 0))],
        out_specs=pl.BlockSpec((tm, d), lambda i: (i, 0)),
        compiler_params=_params(("parallel",)),
        name="combine_norm",
    )(x1, y2, y2, route_w, g.reshape(1, d))


def _rope_tables(seq):
    pos = jnp.arange(seq, dtype=jnp.int32)
    inv = ROPE_THETA ** (-jnp.arange(0, 2 * ROPE_HALF, 2, dtype=F32) / (2 * ROPE_HALF))

    def cs(p):
        ang = p.astype(F32)[:, None] * inv[None, :]
        return jnp.cos(ang), jnp.sin(ang)

    cr, sr = cs(pos // GRID_W)
    cc, sc = cs(pos % GRID_W)
    cp, sp = cs(pos)
    axial = (jnp.concatenate([cr, cr, cc, cc], axis=1), jnp.concatenate([-sr, sr, -sc, sc], axis=1))
    linear = (jnp.concatenate([cp, cp, cp, cp], axis=1), jnp.concatenate([-sp, sp, -sp, sp], axis=1))
    return axial, linear


def kernel(x, g_mix, w_in, q_norm_a, k_norm_a, lam_q1, lam_k1, lam_q2, lam_k2, subln_b, w_branch_a,
           w_branch_b, w_gate, b_gate, w_out, g_ffn, w_router_group, w_router_expert, w_e_gate, w_e_up,
           w_e_down, g_final):
    batch, seq, d = x.shape
    t = batch * seq
    depth = g_mix.shape[0]
    a_q = A_HEADS * HEAD
    a_kv = A_KV_HEADS * HEAD
    b_w = B_HEADS * HEAD
    cuts = [0, a_q, a_q + a_kv, a_q + 2 * a_kv, a_q + 2 * a_kv + b_w, a_q + 2 * a_kv + 2 * b_w,
            a_q + 2 * a_kv + 3 * b_w]
    (cos_a, sin_a), (cos_b, sin_b) = _rope_tables(seq)
    scale_a = HEAD ** -0.5 * math.log2(math.e)
    scale_b = B_QK_DIM ** -0.5 * math.log2(math.e)
    ones = jnp.ones((HEAD,), F32)

    xf = x.reshape(t, d)
    for l in range(depth):
        lam_init = 0.8 - 0.6 * math.exp(-0.3 * l)
        w = w_in[l].astype(BF16)
        w_qa, w_ka, w_va, w_qb, w_kb, w_vb = (w[:, cuts[i]:cuts[i + 1]] for i in range(6))

        h = _rmsnorm(xf, g_mix[l], BF16)
        qa =_proj_rope(h, w_qa, cos_a * scale_a, sin_a * scale_a, q_norm_a[l], normalize=True, seq=seq)
        ka = _proj_rope(h, w_ka, cos_a, sin_a, k_norm_a[l], normalize=True, seq=seq)
        vat = _proj_t(h, w_va.T)
        qb = _proj_rope(h, w_qb, cos_b * scale_b, sin_b * scale_b, ones, normalize=False, seq=seq)
        kb = _proj_rope(h, w_kb, cos_b, sin_b, ones, normalize=False, seq=seq)
        vbt = _proj_t(h, w_vb.T)
        gates = _proj_gate(h, w_gate[l].astype(BF16), b_gate[l])

        o_a = _attention(qa, ka, vat, [], batch=batch, seq=seq, kv_heads=A_KV_HEADS, q_cols=A_GROUP * HEAD,
                         differential=False, lam_init=lam_init)
        lam_rows = [v[l].reshape(1, B_QK_DIM) for v in (lam_q1, lam_k1, lam_q2, lam_k2)]
        o_b = _attention(qb, kb, vbt, lam_rows + [subln_b[l].reshape(HEAD, 1)], batch=batch, seq=seq,
                         kv_heads=B_HEADS, q_cols=HEAD, differential=True, lam_init=lam_init)

        w_r = jnp.zeros((d, ROUTER_LANES), F32)
        w_r = w_r.at[:, :N_GROUPS].set(w_router_group[l]).at[:, N_GROUPS:N_GROUPS + N_EXPERTS].set(w_router_expert[l])
        x1, h2, route_e, route_w = _merge(o_a, o_b, gates, xf, w_branch_a[l].astype(BF16),
                                          w_branch_b[l].astype(BF16), w_out[l].astype(BF16), g_ffn[l],
                                          w_r.astype(BF16))

        block_expert, n_used, src_tok, dst_row = _routing_plan(route_e[:, :TOP_K], t)
        y2 = _moe(block_expert, n_used, src_tok, dst_row, h2, w_e_gate[l].astype(BF16), w_e_up[l].astype(BF16),
                  w_e_down[l].astype(BF16), n_out_rows=TOP_K * t + MOE_BLOCK)
        xf = _final(x1, y2, route_w, g_final, normalize=(l + 1 == depth))
    return xf.reshape(batch, seq, d)
```

```python
import functools
import math

import jax
import jax.numpy as jnp
from jax import lax
from jax.experimental import pallas as pl
from jax.experimental.pallas import tpu as pltpu

F32 = jnp.float32
BF16 = jnp.bfloat16

GRID_W = 64
ROPE_THETA = 10000.0
EPS = 1e-6
LANES = 128
MXU_COLS = 256
HEAD = 128
V_ROWS = HEAD + 16
ROPE_HALF = 32
A_HEADS = 8
A_KV_HEADS = 2
A_GROUP = A_HEADS // A_KV_HEADS
B_HEADS = 8
B_QK_DIM = 64
N_GROUPS = 4
EXPERTS_PER_GROUP = 8
N_EXPERTS = N_GROUPS * EXPERTS_PER_GROUP
TOP_K = 2
ROUTER_LANES = 128

VMEM_LIMIT = 56 * 1024 * 1024

ROW_TILE = 512
MERGE_TILE = 256
SCORE_LANES = 1024
KV_CHUNK = 512
SCORE_SLOTS = 4
QUERY_ROWS_PER_STEP = 2048
MOE_BLOCK = 256
DMA_UNROLL = 8


def _params(sem):
    return pltpu.CompilerParams(dimension_semantics=sem, vmem_limit_bytes=VMEM_LIMIT)


def _slab_pitch(d):
    return d // LANES + 1


def _slab_store(ref, val):
    rows, d = val.shape
    pitch = _slab_pitch(d)
    for s in range(d // LANES):
        ref[pl.ds(s, rows, stride=pitch), :] = val[:, s * LANES:(s + 1) * LANES]
    ref[pl.ds(d // LANES, rows, stride=pitch), :] = jnp.zeros((rows, LANES), val.dtype)


def _slab_load(ref, rows, d):
    pitch = _slab_pitch(d)
    return jnp.concatenate([ref[pl.ds(s, rows, stride=pitch), :] for s in range(d // LANES)], axis=1)


def _rmsnorm_kernel(x_ref, g_ref, o_ref):
    x = x_ref[...]
    ms = jnp.mean(x * x, axis=-1, keepdims=True)
    o_ref[...] = (x * lax.rsqrt(ms + EPS) * g_ref[...]).astype(o_ref.dtype)


def _rmsnorm(x, g, out_dtype):
    t, d = x.shape
    return pl.pallas_call(
        _rmsnorm_kernel,
        out_shape=jax.ShapeDtypeStruct((t, d), out_dtype),
        grid=(t // ROW_TILE,),
        in_specs=[pl.BlockSpec((ROW_TILE, d), lambda i: (i, 0)),
                  pl.BlockSpec((1, d), lambda i: (0, 0))],
        out_specs=pl.BlockSpec((ROW_TILE, d), lambda i: (i, 0)),
        compiler_params=_params(("parallel",)),
        name="rmsnorm",
    )(x, g.reshape(1, d))


def _rotary(t, cos, sin):
    lane = lax.broadcasted_iota(jnp.int32, t.shape, 1)
    first = (lane % (2 * ROPE_HALF)) < ROPE_HALF
    partner = jnp.where(first, pltpu.roll(t, HEAD - ROPE_HALF, 1), pltpu.roll(t, ROPE_HALF, 1))
    return t * cos + partner * sin


def _proj_rope_kernel(h_ref, w_ref, cos_ref, sin_ref, g_ref, o_ref, *, normalize):
    h = h_ref[...]
    cos = cos_ref[...]
    sin = sin_ref[...]
    for c0 in range(0, o_ref.shape[1], MXU_COLS):
        acc = jnp.dot(h, w_ref[:, c0:c0 + MXU_COLS], preferred_element_type=F32)
        for c in range(0, MXU_COLS, HEAD):
            t = acc[:, c:c + HEAD]
            if normalize:
                ms = jnp.mean(t * t, axis=-1, keepdims=True)
                t = t * lax.rsqrt(ms + EPS) * g_ref[...]
            o_ref[:, c0 + c:c0 + c + HEAD] = _rotary(t, cos, sin).astype(o_ref.dtype)


def _proj_rope(h, w, cos, sin, gain, *, normalize, seq):
    t, d = h.shape
    n = w.shape[1]
    tiles_per_seq = seq // ROW_TILE
    return pl.pallas_call(
        functools.partial(_proj_rope_kernel, normalize=normalize),
        out_shape=jax.ShapeDtypeStruct((t, n), BF16),
        grid=(t // ROW_TILE,),
        in_specs=[pl.BlockSpec((ROW_TILE, d), lambda i: (i, 0)),
                  pl.BlockSpec((d, n), lambda i: (0, 0)),
                  pl.BlockSpec((ROW_TILE, HEAD), lambda i: (i % tiles_per_seq, 0)),
                  pl.BlockSpec((ROW_TILE, HEAD), lambda i: (i % tiles_per_seq, 0)),
                  pl.BlockSpec((1, HEAD), lambda i: (0, 0))],
        out_specs=pl.BlockSpec((ROW_TILE, n), lambda i: (i, 0)),
        compiler_params=_params(("parallel",)),
        name="proj_rope",
    )(h, w, cos, sin, gain.reshape(1, HEAD))


def _proj_t_kernel(h_ref, wt_ref, o_ref):
    res = lax.dot_general(wt_ref[...], h_ref[...], (((1,), (1,)), ((), ())), preferred_element_type=F32)
    pad = V_ROWS - HEAD
    row = lax.broadcasted_iota(jnp.int32, (pad, res.shape[1]), 0)
    ones_row = jnp.where(row == 0, 1.0, 0.0).astype(o_ref.dtype)
    for hh in range(res.shape[0] // HEAD):
        o_ref[hh * V_ROWS:hh * V_ROWS + HEAD, :] = res[hh * HEAD:(hh + 1) * HEAD].astype(o_ref.dtype)
        o_ref[hh * V_ROWS + HEAD:(hh + 1) * V_ROWS, :] = ones_row


def _proj_t(h, wt):
    t, d = h.shape
    n = wt.shape[0]
    n_out = n // HEAD * V_ROWS
    return pl.pallas_call(
        _proj_t_kernel,
        out_shape=jax.ShapeDtypeStruct((n_out, t), BF16),
        grid=(t // ROW_TILE,),
        in_specs=[pl.BlockSpec((ROW_TILE, d), lambda i: (i, 0)),
                  pl.BlockSpec((n, d), lambda i: (0, 0))],
        out_specs=pl.BlockSpec((n_out, ROW_TILE), lambda i: (0, i)),
        compiler_params=_params(("parallel",)),
        name="proj_t",
    )(h, wt)


def _proj_gate_kernel(h_ref, w_ref, b_ref, o_ref):
    h = h_ref[...]
    for c0 in range(0, o_ref.shape[1], MXU_COLS):
        z = jnp.dot(h, w_ref[:, c0:c0 + MXU_COLS], preferred_element_type=F32) + b_ref[:, c0:c0 + MXU_COLS]
        o_ref[:, c0:c0 + MXU_COLS] = (1.0 / (1.0 + jnp.exp(-z))).astype(o_ref.dtype)


def _proj_gate(h, w, b):
    t, d = h.shape
    n = w.shape[1]
    tn = 1024
    return pl.pallas_call(
        _proj_gate_kernel,
        out_shape=jax.ShapeDtypeStruct((t, n), BF16),
        grid=(n // tn, t // ROW_TILE),
        in_specs=[pl.BlockSpec((ROW_TILE, d), lambda j, i: (i, 0)),
                  pl.BlockSpec((d, tn), lambda j, i: (0, j)),
                  pl.BlockSpec((1, tn), lambda j, i: (0, j))],
        out_specs=pl.BlockSpec((ROW_TILE, tn), lambda j, i: (i, j)),
        compiler_params=_params(("parallel", "parallel")),
        name="proj_gate",
    )(h, w, b.reshape(1, n))


def _attn_kernel(*refs, differential, lam_init, tq, n_tiles):
    if differential:
        (q_ref, k_ref, vt_ref, lq1_ref, lk1_ref, lq2_ref, lk2_ref, sub_ref, o_ref,
         qt_sc, st_sc, mc_sc, m_sc, acc_sc) = refs
    else:
        q_ref, k_ref, vt_ref, o_ref, qt_sc, st_sc, mc_sc, m_sc, acc_sc = refs
    n_chunks = k_ref.shape[0] // KV_CHUNK
    trips_per_tile = n_chunks // SCORE_SLOTS

    def load_q(t, buf):
        rows = pl.ds(pl.multiple_of(t * tq, tq), tq)
        if differential:
            q = q_ref[rows, :]
            lane = lax.broadcasted_iota(jnp.int32, q.shape, 1)
            zero = jnp.zeros_like(q)
            qs = [jnp.where(lane < B_QK_DIM, q, zero), jnp.where(lane >= B_QK_DIM, q, zero)]
        else:
            qs = [q_ref[rows, g * HEAD:(g + 1) * HEAD] for g in range(q_ref.shape[1] // HEAD)]
        for g, qg in enumerate(qs):
            qt_sc[buf, :, g * tq:(g + 1) * tq] = qg.astype(F32).T.astype(BF16)

    def reset_state():
        m_sc[...] = jnp.full(m_sc.shape, -jnp.inf, F32)
        acc_sc[...] = jnp.zeros(acc_sc.shape, F32)

    def scores(f, slot):
        t = jnp.minimum(f // n_chunks, n_tiles - 1)
        off = pl.multiple_of((f % n_chunks) * KV_CHUNK, KV_CHUNK)
        st = jnp.dot(k_ref[pl.ds(off, KV_CHUNK), :], qt_sc[t % 2], preferred_element_type=F32)
        st_sc[slot] = st
        mc_sc[slot] = jnp.max(st, axis=0, keepdims=True)

    def consume(c, slot):
        off = pl.multiple_of(c * KV_CHUNK, KV_CHUNK)
        st = st_sc[slot]
        m_old = m_sc[...]
        m_new = jnp.maximum(m_old, mc_sc[slot])
        alpha = jnp.exp2(m_old - m_new)
        p = jnp.exp2(st - m_new).astype(BF16)
        acc_sc[...] = alpha * acc_sc[...] + jnp.dot(vt_ref[:, pl.ds(off, KV_CHUNK)], p,
                                                    preferred_element_type=F32)
        m_sc[...] = m_new

    def finalize(t):
        rows = pl.ds(pl.multiple_of(t * tq, tq), tq)
        out = acc_sc[:HEAD, :] / acc_sc[HEAD:HEAD + 1, :]
        if differential:
            lam = (jnp.exp(jnp.sum(lq1_ref[...] * lk1_ref[...], axis=1, keepdims=True))
                   - jnp.exp(jnp.sum(lq2_ref[...] * lk2_ref[...], axis=1, keepdims=True)) + lam_init)
            ot = out[:, :tq] - lam * out[:, tq:]
            ms = jnp.mean(ot * ot, axis=0, keepdims=True)
            ot = ot * lax.rsqrt(ms + EPS) * sub_ref[...] * (1.0 - lam_init)
            o_ref[rows, :] = ot.T.astype(o_ref.dtype)
        else:
            for g in range(q_ref.shape[1] // HEAD):
                o_ref[rows, g * HEAD:(g + 1) * HEAD] = out[:, g * tq:(g + 1) * tq].T.astype(o_ref.dtype)

    def trip(i, carry):
        t = i // trips_per_tile
        k = i % trips_per_tile

        @pl.when(jnp.logical_and(k == 0, t + 1 < n_tiles))
        def _():
            load_q(t + 1, (t + 1) % 2)

        for u in range(SCORE_SLOTS):
            scores(i * SCORE_SLOTS + u + 1, (u + 1) % SCORE_SLOTS)
            consume(k * SCORE_SLOTS + u, u)

        @pl.when(k == trips_per_tile - 1)
        def _():
            finalize(t)
            reset_state()
        return carry

    load_q(0, 0)
    reset_state()
    scores(0, 0)
    lax.fori_loop(0, n_tiles * trips_per_tile, trip, 0)


def _attention(q, k, vt, extra, *, batch, seq, kv_heads, q_cols, differential, lam_init):
    t = q.shape[0]
    assert seq % (KV_CHUNK * SCORE_SLOTS) == 0
    n_maps = 2 if differential else q_cols // HEAD
    tq = SCORE_LANES // n_maps
    rows = QUERY_ROWS_PER_STEP
    n_tiles = rows // tq
    nq = seq // rows
    in_specs = [pl.BlockSpec((rows, q_cols), lambda b, h, i: (b * nq + i, h)),
                pl.BlockSpec((seq, HEAD), lambda b, h, i: (b, h)),
                pl.BlockSpec((V_ROWS, seq), lambda b, h, i: (h, b))]
    in_specs += [pl.BlockSpec(e.shape, lambda b, h, i: (0, 0)) for e in extra]
    return pl.pallas_call(
        functools.partial(_attn_kernel, differential=differential, lam_init=lam_init, tq=tq, n_tiles=n_tiles),
        out_shape=jax.ShapeDtypeStruct((t, kv_heads * q_cols), BF16),
        grid=(batch, kv_heads, nq),
        in_specs=in_specs,
        out_specs=pl.BlockSpec((rows, q_cols), lambda b, h, i: (b * nq + i, h)),
        scratch_shapes=[pltpu.VMEM((2, HEAD, SCORE_LANES), BF16),
                        pltpu.VMEM((SCORE_SLOTS, KV_CHUNK, SCORE_LANES), F32),
                        pltpu.VMEM((SCORE_SLOTS, 1, SCORE_LANES), F32),
                        pltpu.VMEM((1, SCORE_LANES), F32),
                        pltpu.VMEM((V_ROWS, SCORE_LANES), F32)],
        compiler_params=_params(("parallel", "parallel", "parallel")),
        name="diff_attention" if differential else "axial_gqa",
    )(q, k, vt, *extra)


def _merge_kernel(oa_ref, ob_ref, ga_ref, gb_ref, x_ref, wba_ref, wbb_ref, wout_ref, gffn_ref, wr_ref,
                  x1_ref, h2_ref, re_ref, rw_ref):
    pa = jnp.dot(oa_ref[...], wba_ref[...], preferred_element_type=F32)
    pb = jnp.dot(ob_ref[...], wbb_ref[...], preferred_element_type=F32)
    merged = ga_ref[...].astype(F32) * pa + gb_ref[...].astype(F32) * pb
    x1 = x_ref[...] + jnp.dot(merged.astype(BF16), wout_ref[...], preferred_element_type=F32)
    x1_ref[...] = x1
    ms = jnp.mean(x1 * x1, axis=-1, keepdims=True)
    h2 = x1 * lax.rsqrt(ms + EPS) * gffn_ref[...]
    _slab_store(h2_ref, h2)

    logits = jnp.dot(h2.astype(BF16), wr_ref[...], preferred_element_type=F32)
    lane = lax.broadcasted_iota(jnp.int32, logits.shape, 1)
    lane_f = lane.astype(F32)
    neg = jnp.float32(-jnp.inf)
    big = jnp.float32(ROUTER_LANES)

    def first_argmax(vals):
        top = jnp.max(vals, axis=1, keepdims=True)
        idx = jnp.min(jnp.where(vals == top, lane_f, big), axis=1, keepdims=True)
        return top, idx

    g_mask = lane < N_GROUPS
    g_logits = jnp.where(g_mask, logits, neg)
    g_top, g_idx = first_argmax(g_logits)
    g_w = 1.0 / jnp.sum(jnp.where(g_mask, jnp.exp(logits - g_top), 0.0), axis=1, keepdims=True)

    e_lo = g_idx * EXPERTS_PER_GROUP + N_GROUPS
    e_mask = (lane_f >= e_lo) & (lane_f < e_lo + EXPERTS_PER_GROUP)
    e_logits = jnp.where(e_mask, logits, neg)
    v1, i1 = first_argmax(e_logits)
    v2, i2 = first_argmax(jnp.where(lane_f == i1, neg, e_logits))
    d = jnp.exp(v2 - v1)
    w1 = g_w / (1.0 + d)
    w2 = g_w * d / (1.0 + d)
    e1 = (i1 - N_GROUPS).astype(jnp.int32)
    e2 = (i2 - N_GROUPS).astype(jnp.int32)
    re_ref[...] = jnp.where(lane == 0, e1, jnp.where(lane == 1, e2, 0))
    rw_ref[...] = jnp.where(lane == 0, w1, jnp.where(lane == 1, w2, 0.0))


def _merge(oa, ob, gates, x, wba, wbb, wout, gffn, wr):
    t, d = x.shape
    tm = MERGE_TILE
    row = lambda i: (i, 0)
    const = lambda i: (0, 0)
    single = pl.Buffered(1)
    return pl.pallas_call(
        _merge_kernel,
        out_shape=(jax.ShapeDtypeStruct((t, d), F32), jax.ShapeDtypeStruct((t * _slab_pitch(d), LANES), F32),
                   jax.ShapeDtypeStruct((t, ROUTER_LANES), jnp.int32),
                   jax.ShapeDtypeStruct((t, ROUTER_LANES), F32)),
        grid=(t // tm,),
        in_specs=[pl.BlockSpec((tm, oa.shape[1]), row),
                  pl.BlockSpec((tm, ob.shape[1]), row),
                  pl.BlockSpec((tm, d), lambda i: (i, 0)),
                  pl.BlockSpec((tm, d), lambda i: (i, 1)),
                  pl.BlockSpec((tm, d), row),
                  pl.BlockSpec(wba.shape, const, pipeline_mode=single),
                  pl.BlockSpec(wbb.shape, const, pipeline_mode=single),
                  pl.BlockSpec(wout.shape, const, pipeline_mode=single),
                  pl.BlockSpec((1, d), const),
                  pl.BlockSpec(wr.shape, const, pipeline_mode=single)],
        out_specs=(pl.BlockSpec((tm, d), row), pl.BlockSpec((tm * _slab_pitch(d), LANES), row),
                   pl.BlockSpec((tm, ROUTER_LANES), row), pl.BlockSpec((tm, ROUTER_LANES), row)),
        compiler_params=_params(("parallel",)),
        name="merge_router",
    )(oa, ob, gates, gates, x, wba, wbb, wout, gffn.reshape(1, d), wr)


def _moe_kernel(be_ref, used_ref, src_ref, src_next_ref, dst_ref, h_hbm, w1_ref, w3_ref, w2_ref, y_hbm,
                xbuf, obuf, gsem, ssem, *, d):
    del be_ref
    i = pl.program_id(0)
    n_used = used_ref[0]
    data = d // LANES
    pitch = _slab_pitch(d)
    cur = i % 2

    def gather_issue(tok_ref, buf):
        def body(r0, c):
            for u in range(DMA_UNROLL):
                r = r0 * DMA_UNROLL + u
                pltpu.make_async_copy(h_hbm.at[pl.ds(tok_ref[0, 0, r] * pitch, data), :],
                                      xbuf.at[buf, pl.ds(r * pitch, data), :], gsem.at[buf]).start(priority=u % 2)
            return c
        lax.fori_loop(0, MOE_BLOCK // DMA_UNROLL, body, 0)

    def scatter_issue():
        def body(r0, c):
            for u in range(DMA_UNROLL):
                r = r0 * DMA_UNROLL + u
                pltpu.make_async_copy(obuf.at[pl.ds(r * pitch, pitch), :],
                                      y_hbm.at[pl.ds(dst_ref[0, 0, r] * pitch, pitch), :], ssem).start(priority=u % 2)
            return c
        lax.fori_loop(0, MOE_BLOCK // DMA_UNROLL, body, 0)

    def gather_wait(buf):
        n = MOE_BLOCK * data
        pltpu.make_async_copy(h_hbm.at[pl.ds(0, n), :], xbuf.at[buf, pl.ds(0, n), :], gsem.at[buf]).wait()

    def scatter_wait():
        pltpu.make_async_copy(obuf, y_hbm.at[pl.ds(0, MOE_BLOCK * pitch), :], ssem).wait()

    @pl.when(i == 0)
    def _():
        gather_issue(src_ref, 0)
        obuf[...] = jnp.zeros(obuf.shape, F32)
        n_rows = y_hbm.shape[0]
        init = pltpu.make_async_copy(obuf, y_hbm.at[pl.ds(n_rows - MOE_BLOCK * pitch, MOE_BLOCK * pitch), :], ssem)
        init.start()
        init.wait()

    @pl.when(i < n_used)
    def _():
        gather_wait(cur)

        @pl.when(i + 1 < n_used)
        def _():
            gather_issue(src_next_ref, 1 - cur)

        xb = _slab_load(xbuf.at[cur], MOE_BLOCK, d).astype(BF16)
        a1 = jnp.dot(xb, w1_ref[...], preferred_element_type=F32)
        a3 = jnp.dot(xb, w3_ref[...], preferred_element_type=F32)
        u = a1 * (1.0 / (1.0 + jnp.exp(-a1))) * a3
        out = jnp.dot(u.astype(BF16), w2_ref[...], preferred_element_type=F32)

        @pl.when(i > 0)
        def _():
            scatter_wait()

        _slab_store(obuf, out)
        scatter_issue()

        @pl.when(i == n_used - 1)
        def _():
            scatter_wait()


def _moe(block_expert, n_used, src_tok, dst_row, h2_slab, w1, w3, w2, *, n_out_rows):
    n_blocks = src_tok.shape[0]
    d, ff = w1.shape[1], w1.shape[2]
    slab = _slab_pitch(d)
    grid_spec = pltpu.PrefetchScalarGridSpec(
        num_scalar_prefetch=2,
        grid=(n_blocks,),
        in_specs=[pl.BlockSpec((1, 1, MOE_BLOCK), lambda i, be, nu: (i, 0, 0), memory_space=pltpu.SMEM),
                  pl.BlockSpec((1, 1, MOE_BLOCK), lambda i, be, nu: (jnp.minimum(i + 1, n_blocks - 1), 0, 0),
                               memory_space=pltpu.SMEM),
                  pl.BlockSpec((1, 1, MOE_BLOCK), lambda i, be, nu: (i, 0, 0), memory_space=pltpu.SMEM),
                  pl.BlockSpec(memory_space=pl.ANY),
                  pl.BlockSpec((None, d, ff), lambda i, be, nu: (be[i], 0, 0)),
                  pl.BlockSpec((None, d, ff), lambda i, be, nu: (be[i], 0, 0)),
                  pl.BlockSpec((None, ff, d), lambda i, be, nu: (be[i], 0, 0))],
        out_specs=pl.BlockSpec(memory_space=pl.ANY),
        scratch_shapes=[pltpu.VMEM((2, MOE_BLOCK * slab, LANES), F32), pltpu.VMEM((MOE_BLOCK * slab, LANES), F32),
                        pltpu.SemaphoreType.DMA((2,)), pltpu.SemaphoreType.DMA],
    )
    return pl.pallas_call(
        functools.partial(_moe_kernel, d=d),
        out_shape=jax.ShapeDtypeStruct((n_out_rows * slab, LANES), F32),
        grid_spec=grid_spec,
        compiler_params=_params(("arbitrary",)),
        name="expert_mlp",
    )(block_expert, n_used, src_tok, src_tok, dst_row, h2_slab, w1, w3, w2)


def _routing_plan(route_e, n_tokens):
    n_assign = n_tokens * TOP_K
    e_flat = route_e.reshape(n_assign)
    order = jnp.argsort(e_flat).astype(jnp.int32)
    experts = jnp.arange(N_EXPERTS, dtype=jnp.int32)
    counts = jnp.sum((e_flat[:, None] == experts[None, :]).astype(jnp.int32), axis=0)
    offsets = jnp.cumsum(counts) - counts
    blocks_per_expert = (counts + MOE_BLOCK - 1) // MOE_BLOCK
    block_ends = jnp.cumsum(blocks_per_expert)
    n_blocks = n_assign // MOE_BLOCK + N_EXPERTS
    block_id = jnp.arange(n_blocks, dtype=jnp.int32)
    block_expert = jnp.sum((block_ends[None, :] <= block_id[:, None]).astype(jnp.int32), axis=1)
    block_expert = jnp.minimum(block_expert, N_EXPERTS - 1)
    first_block = (block_ends - blocks_per_expert)[block_expert]
    in_block = jnp.arange(MOE_BLOCK, dtype=jnp.int32)[None, :]
    rank = (block_id - first_block)[:, None] * MOE_BLOCK + in_block
    valid = (rank < counts[block_expert][:, None]) & (block_id < block_ends[-1])[:, None]
    assign = order[jnp.clip(offsets[block_expert][:, None] + rank, 0, n_assign - 1)]
    token = assign // TOP_K
    src_tok = jnp.where(valid, token, 0).astype(jnp.int32)
    dst_row = jnp.where(valid, (assign % TOP_K) * n_tokens + token, n_assign + in_block).astype(jnp.int32)
    shape = (n_blocks, 1, MOE_BLOCK)
    return block_expert, block_ends[-1:].astype(jnp.int32), src_tok.reshape(shape), dst_row.reshape(shape)


def _final_kernel(x1_ref, y0_ref, y1_ref, rw_ref, g_ref, o_ref, *, normalize):
    rw = rw_ref[...]
    lane = lax.broadcasted_iota(jnp.int32, rw.shape, 1)
    w1 = jnp.sum(jnp.where(lane == 0, rw, 0.0), axis=1, keepdims=True)
    w2 = jnp.sum(jnp.where(lane == 1, rw, 0.0), axis=1, keepdims=True)
    rows, d = x1_ref.shape
    x2 = x1_ref[...] + (_slab_load(y0_ref, rows, d) * w1 + _slab_load(y1_ref, rows, d) * w2)
    if normalize:
        ms = jnp.mean(x2 * x2, axis=-1, keepdims=True)
        x2 = x2 * lax.rsqrt(ms + EPS) * g_ref[...]
    o_ref[...] = x2


def _final(x1, y2, route_w, g, *, normalize):
    t, d = x1.shape
    tm = MERGE_TILE
    nt = t // tm
    return pl.pallas_call(
        functools.partial(_final_kernel, normalize=normalize),
        out_shape=jax.ShapeDtypeStruct((t, d), F32),
        grid=(nt,),
        in_specs=[pl.BlockSpec((tm, d), lambda i: (i, 0)),
                  pl.BlockSpec((tm * _slab_pitch(d), LANES), lambda i: (i, 0)),
                  pl.BlockSpec((tm * _slab_pitch(d), LANES), lambda i: (i + nt, 0)),
                  pl.BlockSpec((tm, ROUTER_LANES), lambda i: (i, 0)),
                  pl.BlockSpec((1, d), lambda i: (0, 0))],
        out_specs=pl.BlockSpec((tm, d), lambda i: (i, 0)),
        compiler_params=_params(("parallel",)),
        name="combine_norm",
    )(x1, y2, y2, route_w, g.reshape(1, d))


def _rope_tables(seq):
    pos = jnp.arange(seq, dtype=jnp.int32)
    inv = ROPE_THETA ** (-jnp.arange(0, 2 * ROPE_HALF, 2, dtype=F32) / (2 * ROPE_HALF))

    def cs(p):
        ang = p.astype(F32)[:, None] * inv[None, :]
        return jnp.cos(ang), jnp.sin(ang)

    cr, sr = cs(pos // GRID_W)
    cc, sc = cs(pos % GRID_W)
    cp, sp = cs(pos)
    axial = (jnp.concatenate([cr, cr, cc, cc], axis=1), jnp.concatenate([-sr, sr, -sc, sc], axis=1))
    linear = (jnp.concatenate([cp, cp, cp, cp], axis=1), jnp.concatenate([-sp, sp, -sp, sp], axis=1))
    return axial, linear


def kernel(x, g_mix, w_in, q_norm_a, k_norm_a, lam_q1, lam_k1, lam_q2, lam_k2, subln_b, w_branch_a,
           w_branch_b, w_gate, b_gate, w_out, g_ffn, w_router_group, w_router_expert, w_e_gate, w_e_up,
           w_e_down, g_final):
    batch, seq, d = x.shape
    t = batch * seq
    depth = g_mix.shape[0]
    a_q = A_HEADS * HEAD
    a_kv = A_KV_HEADS * HEAD
    b_w = B_HEADS * HEAD
    cuts = [0, a_q, a_q + a_kv, a_q + 2 * a_kv, a_q + 2 * a_kv + b_w, a_q + 2 * a_kv + 2 * b_w,
            a_q + 2 * a_kv + 3 * b_w]
    (cos_a, sin_a), (cos_b, sin_b) = _rope_tables(seq)
    scale_a = HEAD ** -0.5 * math.log2(math.e)
    scale_b = B_QK_DIM ** -0.5 * math.log2(math.e)
    ones = jnp.ones((HEAD,), F32)

    xf = x.reshape(t, d)
    for l in range(depth):
        lam_init = 0.8 - 0.6 * math.exp(-0.3 * l)
        w = w_in[l].astype(BF16)
        w_qa, w_ka, w_va, w_qb, w_kb, w_vb = (w[:, cuts[i]:cuts[i + 1]] for i in range(6))

        h = _rmsnorm(xf, g_mix[l], BF16)
        qa =_proj_rope(h, w_qa, cos_a * scale_a, sin_a * scale_a, q_norm_a[l], normalize=True, seq=seq)
        ka = _proj_rope(h, w_ka, cos_a, sin_a, k_norm_a[l], normalize=True, seq=seq)
        vat = _proj_t(h, w_va.T)
        qb = _proj_rope(h, w_qb, cos_b * scale_b, sin_b * scale_b, ones, normalize=False, seq=seq)
        kb = _proj_rope(h, w_kb, cos_b, sin_b, ones, normalize=False, seq=seq)
        vbt = _proj_t(h, w_vb.T)
        gates = _proj_gate(h, w_gate[l].astype(BF16), b_gate[l])

        o_a = _attention(qa, ka, vat, [], batch=batch, seq=seq, kv_heads=A_KV_HEADS, q_cols=A_GROUP * HEAD,
                         differential=False, lam_init=lam_init)
        lam_rows = [v[l].reshape(1, B_QK_DIM) for v in (lam_q1, lam_k1, lam_q2, lam_k2)]
        o_b = _attention(qb, kb, vbt, lam_rows + [subln_b[l].reshape(HEAD, 1)], batch=batch, seq=seq,
                         kv_heads=B_HEADS, q_cols=HEAD, differential=True, lam_init=lam_init)

        w_r = jnp.zeros((d, ROUTER_LANES), F32)
        w_r = w_r.at[:, :N_GROUPS].set(w_router_group[l]).at[:, N_GROUPS:N_GROUPS + N_EXPERTS].set(w_router_expert[l])
        x1, h2, route_e, route_w = _merge(o_a, o_b, gates, xf, w_branch_a[l].astype(BF16),
                                          w_branch_b[l].astype(BF16), w_out[l].astype(BF16), g_ffn[l],
                                          w_r.astype(BF16))

        block_expert, n_used, src_tok, dst_row = _routing_plan(route_e[:, :TOP_K], t)
        y2 = _moe(block_expert, n_used, src_tok, dst_row, h2, w_e_gate[l].astype(BF16), w_e_up[l].astype(BF16),
                  w_e_down[l].astype(BF16), n_out_rows=TOP_K * t + MOE_BLOCK)
        xf = _final(x1, y2, route_w, g_final, normalize=(l + 1 == depth))
    return xf.reshape(batch, seq, d)
```

```python
import functools
import math

import jax
import jax.numpy as jnp
from jax import lax
from jax.experimental import pallas as pl
from jax.experimental.pallas import tpu as pltpu

F32 = jnp.float32
BF16 = jnp.bfloat16

GRID_W = 64
ROPE_THETA = 10000.0
EPS = 1e-6
LANES = 128
MXU_COLS = 256
HEAD = 128
V_ROWS = HEAD + 16
ROPE_HALF = 32
A_HEADS = 8
A_KV_HEADS = 2
A_GROUP = A_HEADS // A_KV_HEADS
B_HEADS = 8
B_QK_DIM = 64
N_GROUPS = 4
EXPERTS_PER_GROUP = 8
N_EXPERTS = N_GROUPS * EXPERTS_PER_GROUP
TOP_K = 2
ROUTER_LANES = 128

VMEM_LIMIT = 56 * 1024 * 1024

ROW_TILE = 512
MERGE_TILE = 256
SCORE_LANES = 1024
KV_CHUNK = 512
SCORE_SLOTS = 4
QUERY_ROWS_PER_STEP = 4096
MOE_BLOCK = 256
DMA_UNROLL = 8


def _params(sem):
    return pltpu.CompilerParams(dimension_semantics=sem, vmem_limit_bytes=VMEM_LIMIT)


def _slab_pitch(d):
    return d // LANES + 1


def _slab_store(ref, val):
    rows, d = val.shape
    pitch = _slab_pitch(d)
    for s in range(d // LANES):
        ref[pl.ds(s, rows, stride=pitch), :] = val[:, s * LANES:(s + 1) * LANES]
    ref[pl.ds(d // LANES, rows, stride=pitch), :] = jnp.zeros((rows, LANES), val.dtype)


def _slab_load(ref, rows, d):
    pitch = _slab_pitch(d)
    return jnp.concatenate([ref[pl.ds(s, rows, stride=pitch), :] for s in range(d // LANES)], axis=1)


def _rmsnorm_kernel(x_ref, g_ref, o_ref):
    x = x_ref[...]
    ms = jnp.mean(x * x, axis=-1, keepdims=True)
    o_ref[...] = (x * lax.rsqrt(ms + EPS) * g_ref[...]).astype(o_ref.dtype)


def _rmsnorm(x, g, out_dtype):
    t, d = x.shape
    return pl.pallas_call(
        _rmsnorm_kernel,
        out_shape=jax.ShapeDtypeStruct((t, d), out_dtype),
        grid=(t // ROW_TILE,),
        in_specs=[pl.BlockSpec((ROW_TILE, d), lambda i: (i, 0)),
                  pl.BlockSpec((1, d), lambda i: (0, 0))],
        out_specs=pl.BlockSpec((ROW_TILE, d), lambda i: (i, 0)),
        compiler_params=_params(("parallel",)),
        name="rmsnorm",
    )(x, g.reshape(1, d))


def _rotary(t, cos, sin):
    lane = lax.broadcasted_iota(jnp.int32, t.shape, 1)
    first = (lane % (2 * ROPE_HALF)) < ROPE_HALF
    partner = jnp.where(first, pltpu.roll(t, HEAD - ROPE_HALF, 1), pltpu.roll(t, ROPE_HALF, 1))
    return t * cos + partner * sin


def _proj_rope_kernel(h_ref, w_ref, cos_ref, sin_ref, g_ref, o_ref, *, normalize):
    h = h_ref[...]
    cos = cos_ref[...]
    sin = sin_ref[...]
    for c0 in range(0, o_ref.shape[1], MXU_COLS):
        acc = jnp.dot(h, w_ref[:, c0:c0 + MXU_COLS], preferred_element_type=F32)
        for c in range(0, MXU_COLS, HEAD):
            t = acc[:, c:c + HEAD]
            if normalize:
                ms = jnp.mean(t * t, axis=-1, keepdims=True)
                t = t * lax.rsqrt(ms + EPS) * g_ref[...]
            o_ref[:, c0 + c:c0 + c + HEAD] = _rotary(t, cos, sin).astype(o_ref.dtype)


def _proj_rope(h, w, cos, sin, gain, *, normalize, seq):
    t, d = h.shape
    n = w.shape[1]
    tiles_per_seq = seq // ROW_TILE
    return pl.pallas_call(
        functools.partial(_proj_rope_kernel, normalize=normalize),
        out_shape=jax.ShapeDtypeStruct((t, n), BF16),
        grid=(t // ROW_TILE,),
        in_specs=[pl.BlockSpec((ROW_TILE, d), lambda i: (i, 0)),
                  pl.BlockSpec((d, n), lambda i: (0, 0)),
                  pl.BlockSpec((ROW_TILE, HEAD), lambda i: (i % tiles_per_seq, 0)),
                  pl.BlockSpec((ROW_TILE, HEAD), lambda i: (i % tiles_per_seq, 0)),
                  pl.BlockSpec((1, HEAD), lambda i: (0, 0))],
        out_specs=pl.BlockSpec((ROW_TILE, n), lambda i: (i, 0)),
        compiler_params=_params(("parallel",)),
        name="proj_rope",
    )(h, w, cos, sin, gain.reshape(1, HEAD))


def _proj_t_kernel(h_ref, wt_ref, o_ref):
    res = lax.dot_general(wt_ref[...], h_ref[...], (((1,), (1,)), ((), ())), preferred_element_type=F32)
    pad = V_ROWS - HEAD
    row = lax.broadcasted_iota(jnp.int32, (pad, res.shape[1]), 0)
    ones_row = jnp.where(row == 0, 1.0, 0.0).astype(o_ref.dtype)
    for hh in range(res.shape[0] // HEAD):
        o_ref[hh * V_ROWS:hh * V_ROWS + HEAD, :] = res[hh * HEAD:(hh + 1) * HEAD].astype(o_ref.dtype)
        o_ref[hh * V_ROWS + HEAD:(hh + 1) * V_ROWS, :] = ones_row


def _proj_t(h, wt):
    t, d = h.shape
    n = wt.shape[0]
    n_out = n // HEAD * V_ROWS
    return pl.pallas_call(
        _proj_t_kernel,
        out_shape=jax.ShapeDtypeStruct((n_out, t), BF16),
        grid=(t // ROW_TILE,),
        in_specs=[pl.BlockSpec((ROW_TILE, d), lambda i: (i, 0)),
                  pl.BlockSpec((n, d), lambda i: (0, 0))],
        out_specs=pl.BlockSpec((n_out, ROW_TILE), lambda i: (0, i)),
        compiler_params=_params(("parallel",)),
        name="proj_t",
    )(h, wt)


def _proj_gate_kernel(h_ref, w_ref, b_ref, o_ref):
    h = h_ref[...]
    for c0 in range(0, o_ref.shape[1], MXU_COLS):
        z = jnp.dot(h, w_ref[:, c0:c0 + MXU_COLS], preferred_element_type=F32) + b_ref[:, c0:c0 + MXU_COLS]
        o_ref[:, c0:c0 + MXU_COLS] = (1.0 / (1.0 + jnp.exp(-z))).astype(o_ref.dtype)


def _proj_gate(h, w, b):
    t, d = h.shape
    n = w.shape[1]
    tn = 1024
    return pl.pallas_call(
        _proj_gate_kernel,
        out_shape=jax.ShapeDtypeStruct((t, n), BF16),
        grid=(n // tn, t // ROW_TILE),
        in_specs=[pl.BlockSpec((ROW_TILE, d), lambda j, i: (i, 0)),
                  pl.BlockSpec((d, tn), lambda j, i: (0, j)),
                  pl.BlockSpec((1, tn), lambda j, i: (0, j))],
        out_specs=pl.BlockSpec((ROW_TILE, tn), lambda j, i: (i, j)),
        compiler_params=_params(("parallel", "parallel")),
        name="proj_gate",
    )(h, w, b.reshape(1, n))


def _attn_kernel(*refs, differential, lam_init, tq, n_tiles):
    if differential:
        (q_ref, k_ref, vt_ref, lq1_ref, lk1_ref, lq2_ref, lk2_ref, sub_ref, o_ref,
         qt_sc, st_sc, mc_sc, m_sc, acc_sc) = refs
    else:
        q_ref, k_ref, vt_ref, o_ref, qt_sc, st_sc, mc_sc, m_sc, acc_sc = refs
    n_chunks = k_ref.shape[0] // KV_CHUNK
    trips_per_tile = n_chunks // SCORE_SLOTS

    def load_q(t, buf):
        rows = pl.ds(pl.multiple_of(t * tq, tq), tq)
        if differential:
            q = q_ref[rows, :]
            lane = lax.broadcasted_iota(jnp.int32, q.shape, 1)
            zero = jnp.zeros_like(q)
            qs = [jnp.where(lane < B_QK_DIM, q, zero), jnp.where(lane >= B_QK_DIM, q, zero)]
        else:
            qs = [q_ref[rows, g * HEAD:(g + 1) * HEAD] for g in range(q_ref.shape[1] // HEAD)]
        for g, qg in enumerate(qs):
            qt_sc[buf, :, g * tq:(g + 1) * tq] = qg.astype(F32).T.astype(BF16)

    def reset_state():
        m_sc[...] = jnp.full(m_sc.shape, -jnp.inf, F32)
        acc_sc[...] = jnp.zeros(acc_sc.shape, F32)

    def scores(f, slot):
        t = jnp.minimum(f // n_chunks, n_tiles - 1)
        off = pl.multiple_of((f % n_chunks) * KV_CHUNK, KV_CHUNK)
        st = jnp.dot(k_ref[pl.ds(off, KV_CHUNK), :], qt_sc[t % 2], preferred_element_type=F32)
        st_sc[slot] = st
        mc_sc[slot] = jnp.max(st, axis=0, keepdims=True)

    def consume(c, slot):
        off = pl.multiple_of(c * KV_CHUNK, KV_CHUNK)
        st = st_sc[slot]
        m_old = m_sc[...]
        m_new = jnp.maximum(m_old, mc_sc[slot])
        alpha = jnp.exp2(m_old - m_new)
        p = jnp.exp2(st - m_new).astype(BF16)
        acc_sc[...] = alpha * acc_sc[...] + jnp.dot(vt_ref[:, pl.ds(off, KV_CHUNK)], p,
                                                    preferred_element_type=F32)
        m_sc[...] = m_new

    def finalize(t):
        rows = pl.ds(pl.multiple_of(t * tq, tq), tq)
        out = acc_sc[:HEAD, :] / acc_sc[HEAD:HEAD + 1, :]
        if differential:
            lam = (jnp.exp(jnp.sum(lq1_ref[...] * lk1_ref[...], axis=1, keepdims=True))
                   - jnp.exp(jnp.sum(lq2_ref[...] * lk2_ref[...], axis=1, keepdims=True)) + lam_init)
            ot = out[:, :tq] - lam * out[:, tq:]
            ms = jnp.mean(ot * ot, axis=0, keepdims=True)
            ot = ot * lax.rsqrt(ms + EPS) * sub_ref[...] * (1.0 - lam_init)
            o_ref[rows, :] = ot.T.astype(o_ref.dtype)
        else:
            for g in range(q_ref.shape[1] // HEAD):
                o_ref[rows, g * HEAD:(g + 1) * HEAD] = out[:, g * tq:(g + 1) * tq].T.astype(o_ref.dtype)

    def trip(i, carry):
        t = i // trips_per_tile
        k = i % trips_per_tile

        @pl.when(jnp.logical_and(k == 0, t + 1 < n_tiles))
        def _():
            load_q(t + 1, (t + 1) % 2)

        for u in range(SCORE_SLOTS):
            scores(i * SCORE_SLOTS + u + 1, (u + 1) % SCORE_SLOTS)
            consume(k * SCORE_SLOTS + u, u)

        @pl.when(k == trips_per_tile - 1)
        def _():
            finalize(t)
            reset_state()
        return carry

    load_q(0, 0)
    reset_state()
    scores(0, 0)
    lax.fori_loop(0, n_tiles * trips_per_tile, trip, 0)


def _attention(q, k, vt, extra, *, batch, seq, kv_heads, q_cols, differential, lam_init):
    t = q.shape[0]
    assert seq % (KV_CHUNK * SCORE_SLOTS) == 0
    n_maps = 2 if differential else q_cols // HEAD
    tq = SCORE_LANES // n_maps
    rows = QUERY_ROWS_PER_STEP
    n_tiles = rows // tq
    nq = seq // rows
    in_specs = [pl.BlockSpec((rows, q_cols), lambda b, h, i: (b * nq + i, h)),
                pl.BlockSpec((seq, HEAD), lambda b, h, i: (b, h)),
                pl.BlockSpec((V_ROWS, seq), lambda b, h, i: (h, b))]
    in_specs += [pl.BlockSpec(e.shape, lambda b, h, i: (0, 0)) for e in extra]
    return pl.pallas_call(
        functools.partial(_attn_kernel, differential=differential, lam_init=lam_init, tq=tq, n_tiles=n_tiles),
        out_shape=jax.ShapeDtypeStruct((t, kv_heads * q_cols), BF16),
        grid=(batch, kv_heads, nq),
        in_specs=in_specs,
        out_specs=pl.BlockSpec((rows, q_cols), lambda b, h, i: (b * nq + i, h)),
        scratch_shapes=[pltpu.VMEM((2, HEAD, SCORE_LANES), BF16),
                        pltpu.VMEM((SCORE_SLOTS, KV_CHUNK, SCORE_LANES), F32),
                        pltpu.VMEM((SCORE_SLOTS, 1, SCORE_LANES), F32),
                        pltpu.VMEM((1, SCORE_LANES), F32),
                        pltpu.VMEM((V_ROWS, SCORE_LANES), F32)],
        compiler_params=_params(("parallel", "parallel", "parallel")),
        name="diff_attention" if differential else "axial_gqa",
    )(q, k, vt, *extra)


def _merge_kernel(oa_ref, ob_ref, ga_ref, gb_ref, x_ref, wba_ref, wbb_ref, wout_ref, gffn_ref, wr_ref,
                  x1_ref, h2_ref, re_ref, rw_ref):
    pa = jnp.dot(oa_ref[...], wba_ref[...], preferred_element_type=F32)
    pb = jnp.dot(ob_ref[...], wbb_ref[...], preferred_element_type=F32)
    merged = ga_ref[...].astype(F32) * pa + gb_ref[...].astype(F32) * pb
    x1 = x_ref[...] + jnp.dot(merged.astype(BF16), wout_ref[...], preferred_element_type=F32)
    x1_ref[...] = x1
    ms = jnp.mean(x1 * x1, axis=-1, keepdims=True)
    h2 = x1 * lax.rsqrt(ms + EPS) * gffn_ref[...]
    _slab_store(h2_ref, h2)

    logits = jnp.dot(h2.astype(BF16), wr_ref[...], preferred_element_type=F32)
    lane = lax.broadcasted_iota(jnp.int32, logits.shape, 1)
    lane_f = lane.astype(F32)
    neg = jnp.float32(-jnp.inf)
    big = jnp.float32(ROUTER_LANES)

    def first_argmax(vals):
        top = jnp.max(vals, axis=1, keepdims=True)
        idx = jnp.min(jnp.where(vals == top, lane_f, big), axis=1, keepdims=True)
        return top, idx

    g_mask = lane < N_GROUPS
    g_logits = jnp.where(g_mask, logits, neg)
    g_top, g_idx = first_argmax(g_logits)
    g_w = 1.0 / jnp.sum(jnp.where(g_mask, jnp.exp(logits - g_top), 0.0), axis=1, keepdims=True)

    e_lo = g_idx * EXPERTS_PER_GROUP + N_GROUPS
    e_mask = (lane_f >= e_lo) & (lane_f < e_lo + EXPERTS_PER_GROUP)
    e_logits = jnp.where(e_mask, logits, neg)
    v1, i1 = first_argmax(e_logits)
    v2, i2 = first_argmax(jnp.where(lane_f == i1, neg, e_logits))
    d = jnp.exp(v2 - v1)
    w1 = g_w / (1.0 + d)
    w2 = g_w * d / (1.0 + d)
    e1 = (i1 - N_GROUPS).astype(jnp.int32)
    e2 = (i2 - N_GROUPS).astype(jnp.int32)
    re_ref[...] = jnp.where(lane == 0, e1, jnp.where(lane == 1, e2, 0))
    rw_ref[...] = jnp.where(lane == 0, w1, jnp.where(lane == 1, w2, 0.0))


def _merge(oa, ob, gates, x, wba, wbb, wout, gffn, wr):
    t, d = x.shape
    tm = MERGE_TILE
    row = lambda i: (i, 0)
    const = lambda i: (0, 0)
    single = pl.Buffered(1)
    return pl.pallas_call(
        _merge_kernel,
        out_shape=(jax.ShapeDtypeStruct((t, d), F32), jax.ShapeDtypeStruct((t * _slab_pitch(d), LANES), F32),
                   jax.ShapeDtypeStruct((t, ROUTER_LANES), jnp.int32),
                   jax.ShapeDtypeStruct((t, ROUTER_LANES), F32)),
        grid=(t // tm,),
        in_specs=[pl.BlockSpec((tm, oa.shape[1]), row),
                  pl.BlockSpec((tm, ob.shape[1]), row),
                  pl.BlockSpec((tm, d), lambda i: (i, 0)),
                  pl.BlockSpec((tm, d), lambda i: (i, 1)),
                  pl.BlockSpec((tm, d), row),
                  pl.BlockSpec(wba.shape, const, pipeline_mode=single),
                  pl.BlockSpec(wbb.shape, const, pipeline_mode=single),
                  pl.BlockSpec(wout.shape, const, pipeline_mode=single),
                  pl.BlockSpec((1, d), const),
                  pl.BlockSpec(wr.shape, const, pipeline_mode=single)],
        out_specs=(pl.BlockSpec((tm, d), row), pl.BlockSpec((tm * _slab_pitch(d), LANES), row),
                   pl.BlockSpec((tm, ROUTER_LANES), row), pl.BlockSpec((tm, ROUTER_LANES), row)),
        compiler_params=_params(("parallel",)),
        name="merge_router",
    )(oa, ob, gates, gates, x, wba, wbb, wout, gffn.reshape(1, d), wr)


def _moe_kernel(be_ref, used_ref, src_ref, src_next_ref, dst_ref, h_hbm, w1_ref, w3_ref, w2_ref, y_hbm,
                xbuf, obuf, gsem, ssem, *, d, n_blocks):
    del be_ref
    i = pl.program_id(0)
    n_used = used_ref[0]
    data = d // LANES
    pitch = _slab_pitch(d)
    cur = i % 2

    def gather_issue(tok_ref, buf):
        def body(r0, c):
            for u in range(DMA_UNROLL):
                r = r0 * DMA_UNROLL + u
                pltpu.make_async_copy(h_hbm.at[pl.ds(tok_ref[0, 0, r] * pitch, data), :],
                                      xbuf.at[buf, pl.ds(r * pitch, data), :], gsem.at[buf]).start(priority=u % 2)
            return c
        lax.fori_loop(0, MOE_BLOCK // DMA_UNROLL, body, 0)

    def scatter_issue():
        def body(r0, c):
            for u in range(DMA_UNROLL):
                r = r0 * DMA_UNROLL + u
                pltpu.make_async_copy(obuf.at[pl.ds(r * pitch, pitch), :],
                                      y_hbm.at[pl.ds(dst_ref[0, 0, r] * pitch, pitch), :], ssem).start(priority=u % 2)
            return c
        lax.fori_loop(0, MOE_BLOCK // DMA_UNROLL, body, 0)

    def gather_wait(buf):
        n = MOE_BLOCK * data
        pltpu.make_async_copy(h_hbm.at[pl.ds(0, n), :], xbuf.at[buf, pl.ds(0, n), :], gsem.at[buf]).wait()

    def scatter_wait():
        pltpu.make_async_copy(obuf, y_hbm.at[pl.ds(0, MOE_BLOCK * pitch), :], ssem).wait()

    @pl.when(i == 0)
    def _():
        gather_issue(src_ref, 0)
        obuf[...] = jnp.zeros(obuf.shape, F32)
        n_rows = y_hbm.shape[0]
        init = pltpu.make_async_copy(obuf, y_hbm.at[pl.ds(n_rows - MOE_BLOCK * pitch, MOE_BLOCK * pitch), :], ssem)
        init.start()
        init.wait()

    def gather_issue_inline(lo, hi):
        for r in range(lo, hi):
            pltpu.make_async_copy(h_hbm.at[pl.ds(src_next_ref[0, 0, r] * pitch, data), :],
                                  xbuf.at[1 - cur, pl.ds(r * pitch, data), :], gsem.at[1 - cur]).start(priority=r % 2)

    @pl.when(i == n_used)
    def _():
        gather_wait(cur)

    @pl.when(i < n_used)
    def _():
        gather_wait(cur)
        xb = _slab_load(xbuf.at[cur], MOE_BLOCK, d).astype(BF16)
        a1 = jnp.dot(xb, w1_ref[...], preferred_element_type=F32)
        gather_issue_inline(0, MOE_BLOCK // 2)
        a3 = jnp.dot(xb, w3_ref[...], preferred_element_type=F32)
        u = a1 * (1.0 / (1.0 + jnp.exp(-a1))) * a3
        gather_issue_inline(MOE_BLOCK // 2, MOE_BLOCK)
        out = jnp.dot(u.astype(BF16), w2_ref[...], preferred_element_type=F32)

        @pl.when(i == n_blocks - 1)
        def _():
            gather_wait(1 - cur)

        @pl.when(i > 0)
        def _():
            scatter_wait()

        _slab_store(obuf, out)
        scatter_issue()

        @pl.when(i == n_used - 1)
        def _():
            scatter_wait()


def _moe(block_expert, n_used, src_tok, dst_row, h2_slab, w1, w3, w2, *, n_out_rows):
    n_blocks = src_tok.shape[0]
    d, ff = w1.shape[1], w1.shape[2]
    slab = _slab_pitch(d)
    grid_spec = pltpu.PrefetchScalarGridSpec(
        num_scalar_prefetch=2,
        grid=(n_blocks,),
        in_specs=[pl.BlockSpec((1, 1, MOE_BLOCK), lambda i, be, nu: (i, 0, 0), memory_space=pltpu.SMEM),
                  pl.BlockSpec((1, 1, MOE_BLOCK), lambda i, be, nu: (jnp.minimum(i + 1, n_blocks - 1), 0, 0),
                               memory_space=pltpu.SMEM),
                  pl.BlockSpec((1, 1, MOE_BLOCK), lambda i, be, nu: (i, 0, 0), memory_space=pltpu.SMEM),
                  pl.BlockSpec(memory_space=pl.ANY),
                  pl.BlockSpec((None, d, ff), lambda i, be, nu: (be[i], 0, 0)),
                  pl.BlockSpec((None, d, ff), lambda i, be, nu: (be[i], 0, 0)),
                  pl.BlockSpec((None, ff, d), lambda i, be, nu: (be[i], 0, 0))],
        out_specs=pl.BlockSpec(memory_space=pl.ANY),
        scratch_shapes=[pltpu.VMEM((2, MOE_BLOCK * slab, LANES), F32), pltpu.VMEM((MOE_BLOCK * slab, LANES), F32),
                        pltpu.SemaphoreType.DMA((2,)), pltpu.SemaphoreType.DMA],
    )
    return pl.pallas_call(
        functools.partial(_moe_kernel, d=d, n_blocks=n_blocks),
        out_shape=jax.ShapeDtypeStruct((n_out_rows * slab, LANES), F32),
        grid_spec=grid_spec,
        compiler_params=_params(("arbitrary",)),
        name="expert_mlp",
    )(block_expert, n_used, src_tok, src_tok, dst_row, h2_slab, w1, w3, w2)


def _routing_plan(route_e, n_tokens):
    n_assign = n_tokens * TOP_K
    e_flat = route_e.reshape(n_assign)
    order = jnp.argsort(e_flat).astype(jnp.int32)
    experts = jnp.arange(N_EXPERTS, dtype=jnp.int32)
    counts = jnp.sum((e_flat[:, None] == experts[None, :]).astype(jnp.int32), axis=0)
    offsets = jnp.cumsum(counts) - counts
    blocks_per_expert = (counts + MOE_BLOCK - 1) // MOE_BLOCK
    block_ends = jnp.cumsum(blocks_per_expert)
    n_blocks = n_assign // MOE_BLOCK + N_EXPERTS
    block_id = jnp.arange(n_blocks, dtype=jnp.int32)
    block_expert = jnp.sum((block_ends[None, :] <= block_id[:, None]).astype(jnp.int32), axis=1)
    block_expert = jnp.minimum(block_expert, N_EXPERTS - 1)
    first_block = (block_ends - blocks_per_expert)[block_expert]
    in_block = jnp.arange(MOE_BLOCK, dtype=jnp.int32)[None, :]
    rank = (block_id - first_block)[:, None] * MOE_BLOCK + in_block
    valid = (rank < counts[block_expert][:, None]) & (block_id < block_ends[-1])[:, None]
    assign = order[jnp.clip(offsets[block_expert][:, None] + rank, 0, n_assign - 1)]
    token = assign // TOP_K
    src_tok = jnp.where(valid, token, 0).astype(jnp.int32)
    dst_row = jnp.where(valid, (assign % TOP_K) * n_tokens + token, n_assign + in_block).astype(jnp.int32)
    shape = (n_blocks, 1, MOE_BLOCK)
    return block_expert, block_ends[-1:].astype(jnp.int32), src_tok.reshape(shape), dst_row.reshape(shape)


def _final_kernel(x1_ref, y0_ref, y1_ref, rw_ref, g_ref, o_ref, *, normalize):
    rw = rw_ref[...]
    lane = lax.broadcasted_iota(jnp.int32, rw.shape, 1)
    w1 = jnp.sum(jnp.where(lane == 0, rw, 0.0), axis=1, keepdims=True)
    w2 = jnp.sum(jnp.where(lane == 1, rw, 0.0), axis=1, keepdims=True)
    rows, d = x1_ref.shape
    x2 = x1_ref[...] + (_slab_load(y0_ref, rows, d) * w1 + _slab_load(y1_ref, rows, d) * w2)
    if normalize:
        ms = jnp.mean(x2 * x2, axis=-1, keepdims=True)
        x2 = x2 * lax.rsqrt(ms + EPS) * g_ref[...]
    o_ref[...] = x2


def _final(x1, y2, route_w, g, *, normalize):
    t, d = x1.shape
    tm = MERGE_TILE
    nt = t // tm
    return pl.pallas_call(
        functools.partial(_final_kernel, normalize=normalize),
        out_shape=jax.ShapeDtypeStruct((t, d), F32),
        grid=(nt,),
        in_specs=[pl.BlockSpec((tm, d), lambda i: (i, 0)),
                  pl.BlockSpec((tm * _slab_pitch(d), LANES), lambda i: (i, 0)),
                  pl.BlockSpec((tm * _slab_pitch(d), LANES), lambda i: (i + nt, 0)),
                  pl.BlockSpec((tm, ROUTER_LANES), lambda i: (i, 0)),
                  pl.BlockSpec((1, d), lambda i: (0, 0))],
        out_specs=pl.BlockSpec((tm, d), lambda i: (i, 0)),
        compiler_params=_params(("parallel",)),
        name="combine_norm",
    )(x1, y2, y2, route_w, g.reshape(1, d))


def _rope_tables(seq):
    pos = jnp.arange(seq, dtype=jnp.int32)
    inv = ROPE_THETA ** (-jnp.arange(0, 2 * ROPE_HALF, 2, dtype=F32) / (2 * ROPE_HALF))

    def cs(p):
        ang = p.astype(F32)[:, None] * inv[None, :]
        return jnp.cos(ang), jnp.sin(ang)

    cr, sr = cs(pos // GRID_W)
    cc, sc = cs(pos % GRID_W)
    cp, sp = cs(pos)
    axial = (jnp.concatenate([cr, cr, cc, cc], axis=1), jnp.concatenate([-sr, sr, -sc, sc], axis=1))
    linear = (jnp.concatenate([cp, cp, cp, cp], axis=1), jnp.concatenate([-sp, sp, -sp, sp], axis=1))
    return axial, linear


def kernel(x, g_mix, w_in, q_norm_a, k_norm_a, lam_q1, lam_k1, lam_q2, lam_k2, subln_b, w_branch_a,
           w_branch_b, w_gate, b_gate, w_out, g_ffn, w_router_group, w_router_expert, w_e_gate, w_e_up,
           w_e_down, g_final):
    batch, seq, d = x.shape
    t = batch * seq
    depth = g_mix.shape[0]
    a_q = A_HEADS * HEAD
    a_kv = A_KV_HEADS * HEAD
    b_w = B_HEADS * HEAD
    cuts = [0, a_q, a_q + a_kv, a_q + 2 * a_kv, a_q + 2 * a_kv + b_w, a_q + 2 * a_kv + 2 * b_w,
            a_q + 2 * a_kv + 3 * b_w]
    (cos_a, sin_a), (cos_b, sin_b) = _rope_tables(seq)
    scale_a = HEAD ** -0.5 * math.log2(math.e)
    scale_b = B_QK_DIM ** -0.5 * math.log2(math.e)
    ones = jnp.ones((HEAD,), F32)

    xf = x.reshape(t, d)
    for l in range(depth):
        lam_init = 0.8 - 0.6 * math.exp(-0.3 * l)
        w = w_in[l].astype(BF16)
        w_qa, w_ka, w_va, w_qb, w_kb, w_vb = (w[:, cuts[i]:cuts[i + 1]] for i in range(6))

        h = _rmsnorm(xf, g_mix[l], BF16)
        qa =_proj_rope(h, w_qa, cos_a * scale_a, sin_a * scale_a, q_norm_a[l], normalize=True, seq=seq)
        ka = _proj_rope(h, w_ka, cos_a, sin_a, k_norm_a[l], normalize=True, seq=seq)
        vat = _proj_t(h, w_va.T)
        qb = _proj_rope(h, w_qb, cos_b * scale_b, sin_b * scale_b, ones, normalize=False, seq=seq)
        kb = _proj_rope(h, w_kb, cos_b, sin_b, ones, normalize=False, seq=seq)
        vbt = _proj_t(h, w_vb.T)
        gates = _proj_gate(h, w_gate[l].astype(BF16), b_gate[l])

        o_a = _attention(qa, ka, vat, [], batch=batch, seq=seq, kv_heads=A_KV_HEADS, q_cols=A_GROUP * HEAD,
                         differential=False, lam_init=lam_init)
        lam_rows = [v[l].reshape(1, B_QK_DIM) for v in (lam_q1, lam_k1, lam_q2, lam_k2)]
        o_b = _attention(qb, kb, vbt, lam_rows + [subln_b[l].reshape(HEAD, 1)], batch=batch, seq=seq,
                         kv_heads=B_HEADS, q_cols=HEAD, differential=True, lam_init=lam_init)

        w_r = jnp.zeros((d, ROUTER_LANES), F32)
        w_r = w_r.at[:, :N_GROUPS].set(w_router_group[l]).at[:, N_GROUPS:N_GROUPS + N_EXPERTS].set(w_router_expert[l])
        x1, h2, route_e, route_w = _merge(o_a, o_b, gates, xf, w_branch_a[l].astype(BF16),
                                          w_branch_b[l].astype(BF16), w_out[l].astype(BF16), g_ffn[l],
                                          w_r.astype(BF16))

        block_expert, n_used, src_tok, dst_row = _routing_plan(route_e[:, :TOP_K], t)
        y2 = _moe(block_expert, n_used, src_tok, dst_row, h2, w_e_gate[l].astype(BF16), w_e_up[l].astype(BF16),
                  w_e_down[l].astype(BF16), n_out_rows=TOP_K * t + MOE_BLOCK)
        xf = _final(x1, y2, route_w, g_final, normalize=(l + 1 == depth))
    return xf.reshape(batch, seq, d)
```

```python
import functools
import math

import jax
import jax.numpy as jnp
from jax import lax
from jax.experimental import pallas as pl
from jax.experimental.pallas import tpu as pltpu

F32 = jnp.float32
BF16 = jnp.bfloat16

GRID_W = 64
ROPE_THETA = 10000.0
EPS = 1e-6
LANES = 128
MXU_COLS = 256
HEAD = 128
V_ROWS = HEAD + 16
ROPE_HALF = 32
A_HEADS = 8
A_KV_HEADS = 2
A_GROUP = A_HEADS // A_KV_HEADS
B_HEADS = 8
B_QK_DIM = 64
N_GROUPS = 4
EXPERTS_PER_GROUP = 8
N_EXPERTS = N_GROUPS * EXPERTS_PER_GROUP
TOP_K = 2
ROUTER_LANES = 128

VMEM_LIMIT = 56 * 1024 * 1024

ROW_TILE = 512
NORM_ROW_TILE = 256
MERGE_TILE = 256
SCORE_LANES = 1024
KV_CHUNK = 512
SCORE_SLOTS = 4
QUERY_ROWS_PER_STEP = 2048
MOE_BLOCK = 256
DMA_UNROLL = 8


def _params(sem):
    return pltpu.CompilerParams(dimension_semantics=sem, vmem_limit_bytes=VMEM_LIMIT)


def _slab_pitch(d):
    return d // LANES + 1


def _slab_store(ref, val):
    rows, d = val.shape
    pitch = _slab_pitch(d)
    for s in range(d // LANES):
        ref[pl.ds(s, rows, stride=pitch), :] = val[:, s * LANES:(s + 1) * LANES]
    ref[pl.ds(d // LANES, rows, stride=pitch), :] = jnp.zeros((rows, LANES), val.dtype)


def _slab_load(ref, rows, d):
    pitch = _slab_pitch(d)
    return jnp.concatenate([ref[pl.ds(s, rows, stride=pitch), :] for s in range(d // LANES)], axis=1)


def _rmsnorm_kernel(x_ref, g_ref, o_ref):
    x = x_ref[...]
    ms = jnp.mean(x * x, axis=-1, keepdims=True)
    o_ref[...] = (x * lax.rsqrt(ms + EPS) * g_ref[...]).astype(o_ref.dtype)


def _rmsnorm(x, g, out_dtype):
    t, d = x.shape
    return pl.pallas_call(
        _rmsnorm_kernel,
        out_shape=jax.ShapeDtypeStruct((t, d), out_dtype),
        grid=(t // ROW_TILE,),
        in_specs=[pl.BlockSpec((ROW_TILE, d), lambda i: (i, 0)),
                  pl.BlockSpec((1, d), lambda i: (0, 0))],
        out_specs=pl.BlockSpec((ROW_TILE, d), lambda i: (i, 0)),
        compiler_params=_params(("parallel",)),
        name="rmsnorm",
    )(x, g.reshape(1, d))


def _rotary(t, cos, sin):
    lane = lax.broadcasted_iota(jnp.int32, t.shape, 1)
    first = (lane % (2 * ROPE_HALF)) < ROPE_HALF
    partner = jnp.where(first, pltpu.roll(t, HEAD - ROPE_HALF, 1), pltpu.roll(t, ROPE_HALF, 1))
    return t * cos + partner * sin


def _proj_rope_kernel(h_ref, w_ref, cos_ref, sin_ref, g_ref, o_ref, *, normalize):
    h = h_ref[...]
    cos = cos_ref[...]
    sin = sin_ref[...]
    for c0 in range(0, o_ref.shape[1], MXU_COLS):
        acc = jnp.dot(h, w_ref[:, c0:c0 + MXU_COLS], preferred_element_type=F32)
        for c in range(0, MXU_COLS, HEAD):
            t = acc[:, c:c + HEAD]
            if normalize:
                ms = jnp.mean(t * t, axis=-1, keepdims=True)
                t = t * lax.rsqrt(ms + EPS) * g_ref[...]
            o_ref[:, c0 + c:c0 + c + HEAD] = _rotary(t, cos, sin).astype(o_ref.dtype)


def _proj_rope(h, w, cos, sin, gain, *, normalize, seq):
    t, d = h.shape
    n = w.shape[1]
    tm = NORM_ROW_TILE if normalize else ROW_TILE
    tiles_per_seq = seq // tm
    return pl.pallas_call(
        functools.partial(_proj_rope_kernel, normalize=normalize),
        out_shape=jax.ShapeDtypeStruct((t, n), BF16),
        grid=(t // tm,),
        in_specs=[pl.BlockSpec((tm, d), lambda i: (i, 0)),
                  pl.BlockSpec((d, n), lambda i: (0, 0)),
                  pl.BlockSpec((tm, HEAD), lambda i: (i % tiles_per_seq, 0)),
                  pl.BlockSpec((tm, HEAD), lambda i: (i % tiles_per_seq, 0)),
                  pl.BlockSpec((1, HEAD), lambda i: (0, 0))],
        out_specs=pl.BlockSpec((tm, n), lambda i: (i, 0)),
        compiler_params=_params(("parallel",)),
        name="proj_rope",
    )(h, w, cos, sin, gain.reshape(1, HEAD))


def _proj_t_kernel(h_ref, wt_ref, o_ref):
    res = lax.dot_general(wt_ref[...], h_ref[...], (((1,), (1,)), ((), ())), preferred_element_type=F32)
    pad = V_ROWS - HEAD
    row = lax.broadcasted_iota(jnp.int32, (pad, res.shape[1]), 0)
    ones_row = jnp.where(row == 0, 1.0, 0.0).astype(o_ref.dtype)
    for hh in range(res.shape[0] // HEAD):
        o_ref[hh * V_ROWS:hh * V_ROWS + HEAD, :] = res[hh * HEAD:(hh + 1) * HEAD].astype(o_ref.dtype)
        o_ref[hh * V_ROWS + HEAD:(hh + 1) * V_ROWS, :] = ones_row


def _proj_t(h, wt):
    t, d = h.shape
    n = wt.shape[0]
    n_out = n // HEAD * V_ROWS
    return pl.pallas_call(
        _proj_t_kernel,
        out_shape=jax.ShapeDtypeStruct((n_out, t), BF16),
        grid=(t // ROW_TILE,),
        in_specs=[pl.BlockSpec((ROW_TILE, d), lambda i: (i, 0)),
                  pl.BlockSpec((n, d), lambda i: (0, 0))],
        out_specs=pl.BlockSpec((n_out, ROW_TILE), lambda i: (0, i)),
        compiler_params=_params(("parallel",)),
        name="proj_t",
    )(h, wt)


def _proj_gate_kernel(h_ref, w_ref, b_ref, o_ref):
    h = h_ref[...]
    for c0 in range(0, o_ref.shape[1], MXU_COLS):
        z = jnp.dot(h, w_ref[:, c0:c0 + MXU_COLS], preferred_element_type=F32) + b_ref[:, c0:c0 + MXU_COLS]
        o_ref[:, c0:c0 + MXU_COLS] = (1.0 / (1.0 + jnp.exp(-z))).astype(o_ref.dtype)


def _proj_gate(h, w, b):
    t, d = h.shape
    n = w.shape[1]
    tn = 1024
    return pl.pallas_call(
        _proj_gate_kernel,
        out_shape=jax.ShapeDtypeStruct((t, n), BF16),
        grid=(n // tn, t // ROW_TILE),
        in_specs=[pl.BlockSpec((ROW_TILE, d), lambda j, i: (i, 0)),
                  pl.BlockSpec((d, tn), lambda j, i: (0, j)),
                  pl.BlockSpec((1, tn), lambda j, i: (0, j))],
        out_specs=pl.BlockSpec((ROW_TILE, tn), lambda j, i: (i, j)),
        compiler_params=_params(("parallel", "parallel")),
        name="proj_gate",
    )(h, w, b.reshape(1, n))


def _attn_kernel(*refs, differential, lam_init, tq, n_tiles):
    if differential:
        (q_ref, k_ref, vt_ref, lq1_ref, lk1_ref, lq2_ref, lk2_ref, sub_ref, o_ref,
         qt_sc, st_sc, mc_sc, m_sc, acc_sc) = refs
    else:
        q_ref, k_ref, vt_ref, o_ref, qt_sc, st_sc, mc_sc, m_sc, acc_sc = refs
    n_chunks = k_ref.shape[0] // KV_CHUNK
    trips_per_tile = n_chunks // SCORE_SLOTS

    def load_q(t, buf):
        rows = pl.ds(pl.multiple_of(t * tq, tq), tq)
        if differential:
            q = q_ref[rows, :]
            lane = lax.broadcasted_iota(jnp.int32, q.shape, 1)
            zero = jnp.zeros_like(q)
            qs = [jnp.where(lane < B_QK_DIM, q, zero), jnp.where(lane >= B_QK_DIM, q, zero)]
        else:
            qs = [q_ref[rows, g * HEAD:(g + 1) * HEAD] for g in range(q_ref.shape[1] // HEAD)]
        for g, qg in enumerate(qs):
            qt_sc[buf, :, g * tq:(g + 1) * tq] = qg.astype(F32).T.astype(BF16)

    def reset_state():
        m_sc[...] = jnp.full(m_sc.shape, -jnp.inf, F32)
        acc_sc[...] = jnp.zeros(acc_sc.shape, F32)

    def scores(f, slot):
        t = jnp.minimum(f // n_chunks, n_tiles - 1)
        off = pl.multiple_of((f % n_chunks) * KV_CHUNK, KV_CHUNK)
        st = jnp.dot(k_ref[pl.ds(off, KV_CHUNK), :], qt_sc[t % 2], preferred_element_type=F32)
        st_sc[slot] = st
        mc_sc[slot] = jnp.max(st, axis=0, keepdims=True)

    def consume(c, slot):
        off = pl.multiple_of(c * KV_CHUNK, KV_CHUNK)
        st = st_sc[slot]
        m_old = m_sc[...]
        m_new = jnp.maximum(m_old, mc_sc[slot])
        alpha = jnp.exp2(m_old - m_new)
        p = jnp.exp2(st - m_new).astype(BF16)
        acc_sc[...] = alpha * acc_sc[...] + jnp.dot(vt_ref[:, pl.ds(off, KV_CHUNK)], p,
                                                    preferred_element_type=F32)
        m_sc[...] = m_new

    def finalize(t):
        rows = pl.ds(pl.multiple_of(t * tq, tq), tq)
        out = acc_sc[:HEAD, :] / acc_sc[HEAD:HEAD + 1, :]
        if differential:
            lam = (jnp.exp(jnp.sum(lq1_ref[...] * lk1_ref[...], axis=1, keepdims=True))
                   - jnp.exp(jnp.sum(lq2_ref[...] * lk2_ref[...], axis=1, keepdims=True)) + lam_init)
            ot = out[:, :tq] - lam * out[:, tq:]
            ms = jnp.mean(ot * ot, axis=0, keepdims=True)
            ot = ot * lax.rsqrt(ms + EPS) * sub_ref[...] * (1.0 - lam_init)
            o_ref[rows, :] = ot.T.astype(o_ref.dtype)
        else:
            for g in range(q_ref.shape[1] // HEAD):
                o_ref[rows, g * HEAD:(g + 1) * HEAD] = out[:, g * tq:(g + 1) * tq].T.astype(o_ref.dtype)

    def trip(i, carry):
        t = i // trips_per_tile
        k = i % trips_per_tile

        @pl.when(jnp.logical_and(k == 0, t + 1 < n_tiles))
        def _():
            load_q(t + 1, (t + 1) % 2)

        for u in range(SCORE_SLOTS):
            scores(i * SCORE_SLOTS + u + 1, (u + 1) % SCORE_SLOTS)
            consume(k * SCORE_SLOTS + u, u)

        @pl.when(k == trips_per_tile - 1)
        def _():
            finalize(t)
            reset_state()
        return carry

    load_q(0, 0)
    reset_state()
    scores(0, 0)
    lax.fori_loop(0, n_tiles * trips_per_tile, trip, 0)


def _attention(q, k, vt, extra, *, batch, seq, kv_heads, q_cols, differential, lam_init):
    t = q.shape[0]
    assert seq % (KV_CHUNK * SCORE_SLOTS) == 0
    n_maps = 2 if differential else q_cols // HEAD
    tq = SCORE_LANES // n_maps
    rows = QUERY_ROWS_PER_STEP
    n_tiles = rows // tq
    nq = seq // rows
    in_specs = [pl.BlockSpec((rows, q_cols), lambda b, h, i: (b * nq + i, h)),
                pl.BlockSpec((seq, HEAD), lambda b, h, i: (b, h)),
                pl.BlockSpec((V_ROWS, seq), lambda b, h, i: (h, b))]
    in_specs += [pl.BlockSpec(e.shape, lambda b, h, i: (0, 0)) for e in extra]
    return pl.pallas_call(
        functools.partial(_attn_kernel, differential=differential, lam_init=lam_init, tq=tq, n_tiles=n_tiles),
        out_shape=jax.ShapeDtypeStruct((t, kv_heads * q_cols), BF16),
        grid=(batch, kv_heads, nq),
        in_specs=in_specs,
        out_specs=pl.BlockSpec((rows, q_cols), lambda b, h, i: (b * nq + i, h)),
        scratch_shapes=[pltpu.VMEM((2, HEAD, SCORE_LANES), BF16),
                        pltpu.VMEM((SCORE_SLOTS, KV_CHUNK, SCORE_LANES), F32),
                        pltpu.VMEM((SCORE_SLOTS, 1, SCORE_LANES), F32),
                        pltpu.VMEM((1, SCORE_LANES), F32),
                        pltpu.VMEM((V_ROWS, SCORE_LANES), F32)],
        compiler_params=_params(("parallel", "parallel", "parallel")),
        name="diff_attention" if differential else "axial_gqa",
    )(q, k, vt, *extra)


def _merge_kernel(oa_ref, ob_ref, ga_ref, gb_ref, x_ref, wba_ref, wbb_ref, wout_ref, gffn_ref, wr_ref,
                  x1_ref, h2_ref, re_ref, rw_ref):
    pa = jnp.dot(oa_ref[...], wba_ref[...], preferred_element_type=F32)
    pb = jnp.dot(ob_ref[...], wbb_ref[...], preferred_element_type=F32)
    merged = ga_ref[...].astype(F32) * pa + gb_ref[...].astype(F32) * pb
    x1 = x_ref[...] + jnp.dot(merged.astype(BF16), wout_ref[...], preferred_element_type=F32)
    x1_ref[...] = x1
    ms = jnp.mean(x1 * x1, axis=-1, keepdims=True)
    h2 = x1 * lax.rsqrt(ms + EPS) * gffn_ref[...]
    _slab_store(h2_ref, h2)

    logits = jnp.dot(h2.astype(BF16), wr_ref[...], preferred_element_type=F32)
    lane = lax.broadcasted_iota(jnp.int32, logits.shape, 1)
    lane_f = lane.astype(F32)
    neg = jnp.float32(-jnp.inf)
    big = jnp.float32(ROUTER_LANES)

    def first_argmax(vals):
        top = jnp.max(vals, axis=1, keepdims=True)
        idx = jnp.min(jnp.where(vals == top, lane_f, big), axis=1, keepdims=True)
        return top, idx

    g_mask = lane < N_GROUPS
    g_logits = jnp.where(g_mask, logits, neg)
    g_top, g_idx = first_argmax(g_logits)
    g_w = 1.0 / jnp.sum(jnp.where(g_mask, jnp.exp(logits - g_top), 0.0), axis=1, keepdims=True)

    e_lo = g_idx * EXPERTS_PER_GROUP + N_GROUPS
    e_mask = (lane_f >= e_lo) & (lane_f < e_lo + EXPERTS_PER_GROUP)
    e_logits = jnp.where(e_mask, logits, neg)
    v1, i1 = first_argmax(e_logits)
    v2, i2 = first_argmax(jnp.where(lane_f == i1, neg, e_logits))
    d = jnp.exp(v2 - v1)
    w1 = g_w / (1.0 + d)
    w2 = g_w * d / (1.0 + d)
    e1 = (i1 - N_GROUPS).astype(jnp.int32)
    e2 = (i2 - N_GROUPS).astype(jnp.int32)
    re_ref[...] = jnp.where(lane == 0, e1, jnp.where(lane == 1, e2, 0))
    rw_ref[...] = jnp.where(lane == 0, w1, jnp.where(lane == 1, w2, 0.0))


def _merge(oa, ob, gates, x, wba, wbb, wout, gffn, wr):
    t, d = x.shape
    tm = MERGE_TILE
    row = lambda i: (i, 0)
    const = lambda i: (0, 0)
    single = pl.Buffered(1)
    return pl.pallas_call(
        _merge_kernel,
        out_shape=(jax.ShapeDtypeStruct((t, d), F32), jax.ShapeDtypeStruct((t * _slab_pitch(d), LANES), F32),
                   jax.ShapeDtypeStruct((t, ROUTER_LANES), jnp.int32),
                   jax.ShapeDtypeStruct((t, ROUTER_LANES), F32)),
        grid=(t // tm,),
        in_specs=[pl.BlockSpec((tm, oa.shape[1]), row),
                  pl.BlockSpec((tm, ob.shape[1]), row),
                  pl.BlockSpec((tm, d), lambda i: (i, 0)),
                  pl.BlockSpec((tm, d), lambda i: (i, 1)),
                  pl.BlockSpec((tm, d), row),
                  pl.BlockSpec(wba.shape, const, pipeline_mode=single),
                  pl.BlockSpec(wbb.shape, const, pipeline_mode=single),
                  pl.BlockSpec(wout.shape, const, pipeline_mode=single),
                  pl.BlockSpec((1, d), const),
                  pl.BlockSpec(wr.shape, const, pipeline_mode=single)],
        out_specs=(pl.BlockSpec((tm, d), row), pl.BlockSpec((tm * _slab_pitch(d), LANES), row),
                   pl.BlockSpec((tm, ROUTER_LANES), row), pl.BlockSpec((tm, ROUTER_LANES), row)),
        compiler_params=_params(("parallel",)),
        name="merge_router",
    )(oa, ob, gates, gates, x, wba, wbb, wout, gffn.reshape(1, d), wr)


def _moe_kernel(be_ref, used_ref, src_ref, src_next_ref, dst_ref, h_hbm, w1_ref, w3_ref, w2_ref, y_hbm,
                xbuf, obuf, gsem, ssem, *, d):
    del be_ref
    i = pl.program_id(0)
    n_used = used_ref[0]
    data = d // LANES
    pitch = _slab_pitch(d)
    cur = i % 2

    def gather_issue(tok_ref, buf):
        def body(r0, c):
            for u in range(DMA_UNROLL):
                r = r0 * DMA_UNROLL + u
                pltpu.make_async_copy(h_hbm.at[pl.ds(tok_ref[0, 0, r] * pitch, data), :],
                                      xbuf.at[buf, pl.ds(r * pitch, data), :], gsem.at[buf]).start(priority=u % 2)
            return c
        lax.fori_loop(0, MOE_BLOCK // DMA_UNROLL, body, 0)

    def scatter_issue():
        def body(r0, c):
            for u in range(DMA_UNROLL):
                r = r0 * DMA_UNROLL + u
                pltpu.make_async_copy(obuf.at[pl.ds(r * pitch, pitch), :],
                                      y_hbm.at[pl.ds(dst_ref[0, 0, r] * pitch, pitch), :], ssem).start(priority=u % 2)
            return c
        lax.fori_loop(0, MOE_BLOCK // DMA_UNROLL, body, 0)

    def gather_wait(buf):
        n = MOE_BLOCK * data
        pltpu.make_async_copy(h_hbm.at[pl.ds(0, n), :], xbuf.at[buf, pl.ds(0, n), :], gsem.at[buf]).wait()

    def scatter_wait():
        pltpu.make_async_copy(obuf, y_hbm.at[pl.ds(0, MOE_BLOCK * pitch), :], ssem).wait()

    @pl.when(i == 0)
    def _():
        gather_issue(src_ref, 0)
        obuf[...] = jnp.zeros(obuf.shape, F32)
        n_rows = y_hbm.shape[0]
        init = pltpu.make_async_copy(obuf, y_hbm.at[pl.ds(n_rows - MOE_BLOCK * pitch, MOE_BLOCK * pitch), :], ssem)
        init.start()
        init.wait()

    @pl.when(i < n_used)
    def _():
        gather_wait(cur)

        @pl.when(i + 1 < n_used)
        def _():
            gather_issue(src_next_ref, 1 - cur)

        xb = _slab_load(xbuf.at[cur], MOE_BLOCK, d).astype(BF16)
        a1 = jnp.dot(xb, w1_ref[...], preferred_element_type=F32)
        a3 = jnp.dot(xb, w3_ref[...], preferred_element_type=F32)
        u = a1 * (1.0 / (1.0 + jnp.exp(-a1))) * a3
        out = jnp.dot(u.astype(BF16), w2_ref[...], preferred_element_type=F32)

        @pl.when(i > 0)
        def _():
            scatter_wait()

        _slab_store(obuf, out)
        scatter_issue()

        @pl.when(i == n_used - 1)
        def _():
            scatter_wait()


def _moe(block_expert, n_used, src_tok, dst_row, h2_slab, w1, w3, w2, *, n_out_rows):
    n_blocks = src_tok.shape[0]
    d, ff = w1.shape[1], w1.shape[2]
    slab = _slab_pitch(d)
    grid_spec = pltpu.PrefetchScalarGridSpec(
        num_scalar_prefetch=2,
        grid=(n_blocks,),
        in_specs=[pl.BlockSpec((1, 1, MOE_BLOCK), lambda i, be, nu: (i, 0, 0), memory_space=pltpu.SMEM),
                  pl.BlockSpec((1, 1, MOE_BLOCK), lambda i, be, nu: (jnp.minimum(i + 1, n_blocks - 1), 0, 0),
                               memory_space=pltpu.SMEM),
                  pl.BlockSpec((1, 1, MOE_BLOCK), lambda i, be, nu: (i, 0, 0), memory_space=pltpu.SMEM),
                  pl.BlockSpec(memory_space=pl.ANY),
                  pl.BlockSpec((None, d, ff), lambda i, be, nu: (be[i], 0, 0)),
                  pl.BlockSpec((None, d, ff), lambda i, be, nu: (be[i], 0, 0)),
                  pl.BlockSpec((None, ff, d), lambda i, be, nu: (be[i], 0, 0))],
        out_specs=pl.BlockSpec(memory_space=pl.ANY),
        scratch_shapes=[pltpu.VMEM((2, MOE_BLOCK * slab, LANES), F32), pltpu.VMEM((MOE_BLOCK * slab, LANES), F32),
                        pltpu.SemaphoreType.DMA((2,)), pltpu.SemaphoreType.DMA],
    )
    return pl.pallas_call(
        functools.partial(_moe_kernel, d=d),
        out_shape=jax.ShapeDtypeStruct((n_out_rows * slab, LANES), F32),
        grid_spec=grid_spec,
        compiler_params=_params(("arbitrary",)),
        name="expert_mlp",
    )(block_expert, n_used, src_tok, src_tok, dst_row, h2_slab, w1, w3, w2)


def _routing_plan(route_e, n_tokens):
    n_assign = n_tokens * TOP_K
    e_flat = route_e.reshape(n_assign)
    order = jnp.argsort(e_flat).astype(jnp.int32)
    experts = jnp.arange(N_EXPERTS, dtype=jnp.int32)
    counts = jnp.sum((e_flat[:, None] == experts[None, :]).astype(jnp.int32), axis=0)
    offsets = jnp.cumsum(counts) - counts
    blocks_per_expert = (counts + MOE_BLOCK - 1) // MOE_BLOCK
    block_ends = jnp.cumsum(blocks_per_expert)
    n_blocks = n_assign // MOE_BLOCK + N_EXPERTS
    block_id = jnp.arange(n_blocks, dtype=jnp.int32)
    block_expert = jnp.sum((block_ends[None, :] <= block_id[:, None]).astype(jnp.int32), axis=1)
    block_expert = jnp.minimum(block_expert, N_EXPERTS - 1)
    first_block = (block_ends - blocks_per_expert)[block_expert]
    in_block = jnp.arange(MOE_BLOCK, dtype=jnp.int32)[None, :]
    rank = (block_id - first_block)[:, None] * MOE_BLOCK + in_block
    valid = (rank < counts[block_expert][:, None]) & (block_id < block_ends[-1])[:, None]
    assign = order[jnp.clip(offsets[block_expert][:, None] + rank, 0, n_assign - 1)]
    token = assign // TOP_K
    src_tok = jnp.where(valid, token, 0).astype(jnp.int32)
    dst_row = jnp.where(valid, (assign % TOP_K) * n_tokens + token, n_assign + in_block).astype(jnp.int32)
    shape = (n_blocks, 1, MOE_BLOCK)
    return block_expert, block_ends[-1:].astype(jnp.int32), src_tok.reshape(shape), dst_row.reshape(shape)


def _final_kernel(x1_ref, y0_ref, y1_ref, rw_ref, g_ref, o_ref, *, normalize):
    rw = rw_ref[...]
    lane = lax.broadcasted_iota(jnp.int32, rw.shape, 1)
    w1 = jnp.sum(jnp.where(lane == 0, rw, 0.0), axis=1, keepdims=True)
    w2 = jnp.sum(jnp.where(lane == 1, rw, 0.0), axis=1, keepdims=True)
    rows, d = x1_ref.shape
    x2 = x1_ref[...] + (_slab_load(y0_ref, rows, d) * w1 + _slab_load(y1_ref, rows, d) * w2)
    if normalize:
        ms = jnp.mean(x2 * x2, axis=-1, keepdims=True)
        x2 = x2 * lax.rsqrt(ms + EPS) * g_ref[...]
    o_ref[...] = x2


def _final(x1, y2, route_w, g, *, normalize):
    t, d = x1.shape
    tm = MERGE_TILE
    nt = t // tm
    return pl.pallas_call(
        functools.partial(_final_kernel, normalize=normalize),
        out_shape=jax.ShapeDtypeStruct((t, d), F32),
        grid=(nt,),
        in_specs=[pl.BlockSpec((tm, d), lambda i: (i, 0)),
                  pl.BlockSpec((tm * _slab_pitch(d), LANES), lambda i: (i, 0)),
                  pl.BlockSpec((tm * _slab_pitch(d), LANES), lambda i: (i + nt, 0)),
                  pl.BlockSpec((tm, ROUTER_LANES), lambda i: (i, 0)),
                  pl.BlockSpec((1, d), lambda i: (0, 0))],
        out_specs=pl.BlockSpec((tm, d), lambda i: (i, 0)),
        compiler_params=_params(("parallel",)),
        name="combine_norm",
    )(x1, y2, y2, route_w, g.reshape(1, d))


def _rope_tables(seq):
    pos = jnp.arange(seq, dtype=jnp.int32)
    inv = ROPE_THETA ** (-jnp.arange(0, 2 * ROPE_HALF, 2, dtype=F32) / (2 * ROPE_HALF))

    def cs(p):
        ang = p.astype(F32)[:, None] * inv[None, :]
        return jnp.cos(ang), jnp.sin(ang)

    cr, sr = cs(pos // GRID_W)
    cc, sc = cs(pos % GRID_W)
    cp, sp = cs(pos)
    axial = (jnp.concatenate([cr, cr, cc, cc], axis=1), jnp.concatenate([-sr, sr, -sc, sc], axis=1))
    linear = (jnp.concatenate([cp, cp, cp, cp], axis=1), jnp.concatenate([-sp, sp, -sp, sp], axis=1))
    return axial, linear


def kernel(x, g_mix, w_in, q_norm_a, k_norm_a, lam_q1, lam_k1, lam_q2, lam_k2, subln_b, w_branch_a,
           w_branch_b, w_gate, b_gate, w_out, g_ffn, w_router_group, w_router_expert, w_e_gate, w_e_up,
           w_e_down, g_final):
    batch, seq, d = x.shape
    t = batch * seq
    depth = g_mix.shape[0]
    a_q = A_HEADS * HEAD
    a_kv = A_KV_HEADS * HEAD
    b_w = B_HEADS * HEAD
    cuts = [0, a_q, a_q + a_kv, a_q + 2 * a_kv, a_q + 2 * a_kv + b_w, a_q + 2 * a_kv + 2 * b_w,
            a_q + 2 * a_kv + 3 * b_w]
    (cos_a, sin_a), (cos_b, sin_b) = _rope_tables(seq)
    scale_a = HEAD ** -0.5 * math.log2(math.e)
    scale_b = B_QK_DIM ** -0.5 * math.log2(math.e)
    ones = jnp.ones((HEAD,), F32)

    xf = x.reshape(t, d)
    for l in range(depth):
        lam_init = 0.8 - 0.6 * math.exp(-0.3 * l)
        w = w_in[l].astype(BF16)
        w_qa, w_ka, w_va, w_qb, w_kb, w_vb = (w[:, cuts[i]:cuts[i + 1]] for i in range(6))

        h = _rmsnorm(xf, g_mix[l], BF16)
        qa =_proj_rope(h, w_qa, cos_a * scale_a, sin_a * scale_a, q_norm_a[l], normalize=True, seq=seq)
        ka = _proj_rope(h, w_ka, cos_a, sin_a, k_norm_a[l], normalize=True, seq=seq)
        vat = _proj_t(h, w_va.T)
        qb = _proj_rope(h, w_qb, cos_b * scale_b, sin_b * scale_b, ones, normalize=False, seq=seq)
        kb = _proj_rope(h, w_kb, cos_b, sin_b, ones, normalize=False, seq=seq)
        vbt = _proj_t(h, w_vb.T)
        gates = _proj_gate(h, w_gate[l].astype(BF16), b_gate[l])

        o_a = _attention(qa, ka, vat, [], batch=batch, seq=seq, kv_heads=A_KV_HEADS, q_cols=A_GROUP * HEAD,
                         differential=False, lam_init=lam_init)
        lam_rows = [v[l].reshape(1, B_QK_DIM) for v in (lam_q1, lam_k1, lam_q2, lam_k2)]
        o_b = _attention(qb, kb, vbt, lam_rows + [subln_b[l].reshape(HEAD, 1)], batch=batch, seq=seq,
                         kv_heads=B_HEADS, q_cols=HEAD, differential=True, lam_init=lam_init)

        w_r = jnp.zeros((d, ROUTER_LANES), F32)
        w_r = w_r.at[:, :N_GROUPS].set(w_router_group[l]).at[:, N_GROUPS:N_GROUPS + N_EXPERTS].set(w_router_expert[l])
        x1, h2, route_e, route_w = _merge(o_a, o_b, gates, xf, w_branch_a[l].astype(BF16),
                                          w_branch_b[l].astype(BF16), w_out[l].astype(BF16), g_ffn[l],
                                          w_r.astype(BF16))

        block_expert, n_used, src_tok, dst_row = _routing_plan(route_e[:, :TOP_K], t)
        y2 = _moe(block_expert, n_used, src_tok, dst_row, h2, w_e_gate[l].astype(BF16), w_e_up[l].astype(BF16),
                  w_e_down[l].astype(BF16), n_out_rows=TOP_K * t + MOE_BLOCK)
        xf = _final(x1, y2, route_w, g_final, normalize=(l + 1 == depth))
    return xf.reshape(batch, seq, d)
```

```python
import functools
import math

import jax
import jax.numpy as jnp
from jax import lax
from jax.experimental import pallas as pl
from jax.experimental.pallas import tpu as pltpu

F32 = jnp.float32
BF16 = jnp.bfloat16

GRID_W = 64
ROPE_THETA = 10000.0
EPS = 1e-6
LANES = 128
MXU_COLS = 256
HEAD = 128
V_ROWS = HEAD + 16
ROPE_HALF = 32
A_HEADS = 8
A_KV_HEADS = 2
A_GROUP = A_HEADS // A_KV_HEADS
B_HEADS = 8
B_QK_DIM = 64
N_GROUPS = 4
EXPERTS_PER_GROUP = 8
N_EXPERTS = N_GROUPS * EXPERTS_PER_GROUP
TOP_K = 2
ROUTER_LANES = 128

VMEM_LIMIT = 56 * 1024 * 1024

ROW_TILE = 512
MERGE_TILE = 256
SCORE_LANES = 1024
KV_CHUNK = 512
SCORE_SLOTS = 4
QUERY_ROWS_PER_STEP = 2048
MOE_BLOCK = 256
DMA_UNROLL = 8


def _params(sem):
    return pltpu.CompilerParams(dimension_semantics=sem, vmem_limit_bytes=VMEM_LIMIT)


def _slab_pitch(d):
    return d // LANES + 1


def _slab_store(ref, val):
    rows, d = val.shape
    pitch = _slab_pitch(d)
    for s in range(d // LANES):
        ref[pl.ds(s, rows, stride=pitch), :] = val[:, s * LANES:(s + 1) * LANES]
    ref[pl.ds(d // LANES, rows, stride=pitch), :] = jnp.zeros((rows, LANES), val.dtype)


def _slab_load(ref, rows, d):
    pitch = _slab_pitch(d)
    return jnp.concatenate([ref[pl.ds(s, rows, stride=pitch), :] for s in range(d // LANES)], axis=1)


def _rmsnorm_kernel(x_ref, g_ref, o_ref):
    x = x_ref[...]
    ms = jnp.mean(x * x, axis=-1, keepdims=True)
    o_ref[...] = (x * lax.rsqrt(ms + EPS) * g_ref[...]).astype(o_ref.dtype)


def _rmsnorm(x, g, out_dtype):
    t, d = x.shape
    return pl.pallas_call(
        _rmsnorm_kernel,
        out_shape=jax.ShapeDtypeStruct((t, d), out_dtype),
        grid=(t // ROW_TILE,),
        in_specs=[pl.BlockSpec((ROW_TILE, d), lambda i: (i, 0)),
                  pl.BlockSpec((1, d), lambda i: (0, 0))],
        out_specs=pl.BlockSpec((ROW_TILE, d), lambda i: (i, 0)),
        compiler_params=_params(("parallel",)),
        name="rmsnorm",
    )(x, g.reshape(1, d))


def _rotary(t, cos, sin):
    lane = lax.broadcasted_iota(jnp.int32, t.shape, 1)
    first = (lane % (2 * ROPE_HALF)) < ROPE_HALF
    partner = jnp.where(first, pltpu.roll(t, HEAD - ROPE_HALF, 1), pltpu.roll(t, ROPE_HALF, 1))
    return t * cos + partner * sin


def _proj_rope_kernel(h_ref, w_ref, cos_ref, sin_ref, g_ref, o_ref, *, normalize):
    h = h_ref[...]
    cos = cos_ref[...]
    sin = sin_ref[...]
    for c0 in range(0, o_ref.shape[1], MXU_COLS):
        acc = jnp.dot(h, w_ref[:, c0:c0 + MXU_COLS], preferred_element_type=F32)
        for c in range(0, MXU_COLS, HEAD):
            t = acc[:, c:c + HEAD]
            if normalize:
                ms = jnp.mean(t * t, axis=-1, keepdims=True)
                t = t * lax.rsqrt(ms + EPS) * g_ref[...]
            o_ref[:, c0 + c:c0 + c + HEAD] = _rotary(t, cos, sin).astype(o_ref.dtype)


def _proj_rope(h, w, cos, sin, gain, *, normalize, seq):
    t, d = h.shape
    n = w.shape[1]
    tiles_per_seq = seq // ROW_TILE
    return pl.pallas_call(
        functools.partial(_proj_rope_kernel, normalize=normalize),
        out_shape=jax.ShapeDtypeStruct((t, n), BF16),
        grid=(t // ROW_TILE,),
        in_specs=[pl.BlockSpec((ROW_TILE, d), lambda i: (i, 0)),
                  pl.BlockSpec((d, n), lambda i: (0, 0)),
                  pl.BlockSpec((ROW_TILE, HEAD), lambda i: (i % tiles_per_seq, 0)),
                  pl.BlockSpec((ROW_TILE, HEAD), lambda i: (i % tiles_per_seq, 0)),
                  pl.BlockSpec((1, HEAD), lambda i: (0, 0))],
        out_specs=pl.BlockSpec((ROW_TILE, n), lambda i: (i, 0)),
        compiler_params=_params(("parallel",)),
        name="proj_rope",
    )(h, w, cos, sin, gain.reshape(1, HEAD))


def _proj_t_kernel(h_ref, wt_ref, o_ref):
    res = lax.dot_general(wt_ref[...], h_ref[...], (((1,), (1,)), ((), ())), preferred_element_type=F32)
    pad = V_ROWS - HEAD
    row = lax.broadcasted_iota(jnp.int32, (pad, res.shape[1]), 0)
    ones_row = jnp.where(row == 0, 1.0, 0.0).astype(o_ref.dtype)
    for hh in range(res.shape[0] // HEAD):
        o_ref[hh * V_ROWS:hh * V_ROWS + HEAD, :] = res[hh * HEAD:(hh + 1) * HEAD].astype(o_ref.dtype)
        o_ref[hh * V_ROWS + HEAD:(hh + 1) * V_ROWS, :] = ones_row


def _proj_t(h, wt):
    t, d = h.shape
    n = wt.shape[0]
    n_out = n // HEAD * V_ROWS
    return pl.pallas_call(
        _proj_t_kernel,
        out_shape=jax.ShapeDtypeStruct((n_out, t), BF16),
        grid=(t // ROW_TILE,),
        in_specs=[pl.BlockSpec((ROW_TILE, d), lambda i: (i, 0)),
                  pl.BlockSpec((n, d), lambda i: (0, 0))],
        out_specs=pl.BlockSpec((n_out, ROW_TILE), lambda i: (0, i)),
        compiler_params=_params(("parallel",)),
        name="proj_t",
    )(h, wt)


def _proj_gate_kernel(h_ref, w_ref, b_ref, o_ref):
    h = h_ref[...]
    for c0 in range(0, o_ref.shape[1], MXU_COLS):
        z = jnp.dot(h, w_ref[:, c0:c0 + MXU_COLS], preferred_element_type=F32) + b_ref[:, c0:c0 + MXU_COLS]
        o_ref[:, c0:c0 + MXU_COLS] = (1.0 / (1.0 + jnp.exp(-z))).astype(o_ref.dtype)


def _proj_gate(h, w, b):
    t, d = h.shape
    n = w.shape[1]
    tn = 1024
    return pl.pallas_call(
        _proj_gate_kernel,
        out_shape=jax.ShapeDtypeStruct((t, n), BF16),
        grid=(n // tn, t // ROW_TILE),
        in_specs=[pl.BlockSpec((ROW_TILE, d), lambda j, i: (i, 0)),
                  pl.BlockSpec((d, tn), lambda j, i: (0, j)),
                  pl.BlockSpec((1, tn), lambda j, i: (0, j))],
        out_specs=pl.BlockSpec((ROW_TILE, tn), lambda j, i: (i, j)),
        compiler_params=_params(("parallel", "parallel")),
        name="proj_gate",
    )(h, w, b.reshape(1, n))


def _attn_kernel(*refs, differential, lam_init, tq, n_tiles):
    if differential:
        (q_ref, k_ref, vt_ref, lq1_ref, lk1_ref, lq2_ref, lk2_ref, sub_ref, o_ref,
         qt_sc, st_sc, mc_sc, m_sc, acc_sc) = refs
    else:
        q_ref, k_ref, vt_ref, o_ref, qt_sc, st_sc, mc_sc, m_sc, acc_sc = refs
    n_chunks = k_ref.shape[0] // KV_CHUNK
    trips_per_tile = n_chunks // SCORE_SLOTS

    def load_q(t, buf):
        rows = pl.ds(pl.multiple_of(t * tq, tq), tq)
        if differential:
            q = q_ref[rows, :]
            lane = lax.broadcasted_iota(jnp.int32, q.shape, 1)
            zero = jnp.zeros_like(q)
            qs = [jnp.where(lane < B_QK_DIM, q, zero), jnp.where(lane >= B_QK_DIM, q, zero)]
        else:
            qs = [q_ref[rows, g * HEAD:(g + 1) * HEAD] for g in range(q_ref.shape[1] // HEAD)]
        for g, qg in enumerate(qs):
            qt_sc[buf, :, g * tq:(g + 1) * tq] = qg.astype(F32).T.astype(BF16)

    def reset_state():
        m_sc[...] = jnp.full(m_sc.shape, -jnp.inf, F32)
        acc_sc[...] = jnp.zeros(acc_sc.shape, F32)

    def scores(f, slot):
        t = jnp.minimum(f // n_chunks, n_tiles - 1)
        off = pl.multiple_of((f % n_chunks) * KV_CHUNK, KV_CHUNK)
        st = jnp.dot(k_ref[pl.ds(off, KV_CHUNK), :], qt_sc[t % 2], preferred_element_type=F32)
        st_sc[slot] = st
        mc_sc[slot] = jnp.max(st, axis=0, keepdims=True)

    def consume(c, slot):
        off = pl.multiple_of(c * KV_CHUNK, KV_CHUNK)
        st = st_sc[slot]
        m_old = m_sc[...]
        m_new = jnp.maximum(m_old, mc_sc[slot])
        alpha = jnp.exp2(m_old - m_new)
        p = jnp.exp2(st - m_new).astype(BF16)
        acc_sc[...] = alpha * acc_sc[...] + jnp.dot(vt_ref[:, pl.ds(off, KV_CHUNK)], p,
                                                    preferred_element_type=F32)
        m_sc[...] = m_new

    def finalize(t):
        rows = pl.ds(pl.multiple_of(t * tq, tq), tq)
        out = acc_sc[:HEAD, :] / acc_sc[HEAD:HEAD + 1, :]
        if differential:
            lam = (jnp.exp(jnp.sum(lq1_ref[...] * lk1_ref[...], axis=1, keepdims=True))
                   - jnp.exp(jnp.sum(lq2_ref[...] * lk2_ref[...], axis=1, keepdims=True)) + lam_init)
            ot = out[:, :tq] - lam * out[:, tq:]
            ms = jnp.mean(ot * ot, axis=0, keepdims=True)
            ot = ot * lax.rsqrt(ms + EPS) * sub_ref[...] * (1.0 - lam_init)
            o_ref[rows, :] = ot.T.astype(o_ref.dtype)
        else:
            for g in range(q_ref.shape[1] // HEAD):
                o_ref[rows, g * HEAD:(g + 1) * HEAD] = out[:, g * tq:(g + 1) * tq].T.astype(o_ref.dtype)

    def trip(i, carry):
        t = i // trips_per_tile
        k = i % trips_per_tile

        @pl.when(jnp.logical_and(k == 0, t + 1 < n_tiles))
        def _():
            load_q(t + 1, (t + 1) % 2)

        for u in range(SCORE_SLOTS):
            scores(i * SCORE_SLOTS + u + 1, (u + 1) % SCORE_SLOTS)
            consume(k * SCORE_SLOTS + u, u)

        @pl.when(k == trips_per_tile - 1)
        def _():
            finalize(t)
            reset_state()
        return carry

    load_q(0, 0)
    reset_state()
    scores(0, 0)
    lax.fori_loop(0, n_tiles * trips_per_tile, trip, 0)


def _attention(q, k, vt, extra, *, batch, seq, kv_heads, q_cols, differential, lam_init):
    t = q.shape[0]
    assert seq % (KV_CHUNK * SCORE_SLOTS) == 0
    n_maps = 2 if differential else q_cols // HEAD
    tq = SCORE_LANES // n_maps
    rows = QUERY_ROWS_PER_STEP
    n_tiles = rows // tq
    nq = seq // rows
    in_specs = [pl.BlockSpec((rows, q_cols), lambda b, h, i: (b * nq + i, h)),
                pl.BlockSpec((seq, HEAD), lambda b, h, i: (b, h)),
                pl.BlockSpec((V_ROWS, seq), lambda b, h, i: (h, b))]
    in_specs += [pl.BlockSpec(e.shape, lambda b, h, i: (0, 0)) for e in extra]
    return pl.pallas_call(
        functools.partial(_attn_kernel, differential=differential, lam_init=lam_init, tq=tq, n_tiles=n_tiles),
        out_shape=jax.ShapeDtypeStruct((t, kv_heads * q_cols), BF16),
        grid=(batch, kv_heads, nq),
        in_specs=in_specs,
        out_specs=pl.BlockSpec((rows, q_cols), lambda b, h, i: (b * nq + i, h)),
        scratch_shapes=[pltpu.VMEM((2, HEAD, SCORE_LANES), BF16),
                        pltpu.VMEM((SCORE_SLOTS, KV_CHUNK, SCORE_LANES), F32),
                        pltpu.VMEM((SCORE_SLOTS, 1, SCORE_LANES), F32),
                        pltpu.VMEM((1, SCORE_LANES), F32),
                        pltpu.VMEM((V_ROWS, SCORE_LANES), F32)],
        compiler_params=_params(("parallel", "parallel", "parallel")),
        name="diff_attention" if differential else "axial_gqa",
    )(q, k, vt, *extra)


def _merge_kernel(oa_ref, ob_ref, ga_ref, gb_ref, x_ref, wba_ref, wbb_ref, wout_ref, gffn_ref, wr_ref,
                  x1_ref, h2_ref, re_ref, rw_ref):
    pa = jnp.dot(oa_ref[...], wba_ref[...], preferred_element_type=F32)
    pb = jnp.dot(ob_ref[...], wbb_ref[...], preferred_element_type=F32)
    merged = ga_ref[...].astype(F32) * pa + gb_ref[...].astype(F32) * pb
    x1 = x_ref[...] + jnp.dot(merged.astype(BF16), wout_ref[...], preferred_element_type=F32)
    x1_ref[...] = x1
    ms = jnp.mean(x1 * x1, axis=-1, keepdims=True)
    h2 = x1 * lax.rsqrt(ms + EPS) * gffn_ref[...]
    _slab_store(h2_ref, h2)

    logits = jnp.dot(h2.astype(BF16), wr_ref[...], preferred_element_type=F32)
    lane = lax.broadcasted_iota(jnp.int32, logits.shape, 1)
    lane_f = lane.astype(F32)
    neg = jnp.float32(-jnp.inf)
    big = jnp.float32(ROUTER_LANES)

    def first_argmax(vals):
        top = jnp.max(vals, axis=1, keepdims=True)
        idx = jnp.min(jnp.where(vals == top, lane_f, big), axis=1, keepdims=True)
        return top, idx

    g_mask = lane < N_GROUPS
    g_logits = jnp.where(g_mask, logits, neg)
    g_top, g_idx = first_argmax(g_logits)
    g_w = 1.0 / jnp.sum(jnp.where(g_mask, jnp.exp(logits - g_top), 0.0), axis=1, keepdims=True)

    e_lo = g_idx * EXPERTS_PER_GROUP + N_GROUPS
    e_mask = (lane_f >= e_lo) & (lane_f < e_lo + EXPERTS_PER_GROUP)
    e_logits = jnp.where(e_mask, logits, neg)
    v1, i1 = first_argmax(e_logits)
    v2, i2 = first_argmax(jnp.where(lane_f == i1, neg, e_logits))
    d = jnp.exp(v2 - v1)
    w1 = g_w / (1.0 + d)
    w2 = g_w * d / (1.0 + d)
    e1 = (i1 - N_GROUPS).astype(jnp.int32)
    e2 = (i2 - N_GROUPS).astype(jnp.int32)
    re_ref[...] = jnp.where(lane == 0, e1, jnp.where(lane == 1, e2, 0))
    rw_ref[...] = jnp.where(lane == 0, w1, jnp.where(lane == 1, w2, 0.0))


def _merge(oa, ob, gates, x, wba, wbb, wout, gffn, wr):
    t, d = x.shape
    tm = MERGE_TILE
    row = lambda i: (i, 0)
    const = lambda i: (0, 0)
    single = pl.Buffered(1)
    return pl.pallas_call(
        _merge_kernel,
        out_shape=(jax.ShapeDtypeStruct((t, d), F32), jax.ShapeDtypeStruct((t * _slab_pitch(d), LANES), F32),
                   jax.ShapeDtypeStruct((t, ROUTER_LANES), jnp.int32),
                   jax.ShapeDtypeStruct((t, ROUTER_LANES), F32)),
        grid=(t // tm,),
        in_specs=[pl.BlockSpec((tm, oa.shape[1]), row),
                  pl.BlockSpec((tm, ob.shape[1]), row),
                  pl.BlockSpec((tm, d), lambda i: (i, 0)),
                  pl.BlockSpec((tm, d), lambda i: (i, 1)),
                  pl.BlockSpec((tm, d), row),
                  pl.BlockSpec(wba.shape, const, pipeline_mode=single),
                  pl.BlockSpec(wbb.shape, const, pipeline_mode=single),
                  pl.BlockSpec(wout.shape, const, pipeline_mode=single),
                  pl.BlockSpec((1, d), const),
                  pl.BlockSpec(wr.shape, const, pipeline_mode=single)],
        out_specs=(pl.BlockSpec((tm, d), row), pl.BlockSpec((tm * _slab_pitch(d), LANES), row),
                   pl.BlockSpec((tm, ROUTER_LANES), row), pl.BlockSpec((tm, ROUTER_LANES), row)),
        compiler_params=_params(("parallel",)),
        name="merge_router",
    )(oa, ob, gates, gates, x, wba, wbb, wout, gffn.reshape(1, d), wr)


def _moe_kernel(be_ref, used_ref, src_ref, src_next_ref, dst_ref, h_hbm, w1_ref, w3_ref, w2_ref, y_hbm,
                xbuf, obuf, w1b, w3b, w2b, gsem, ssem, *, d):
    i = pl.program_id(0)
    n_used = used_ref[0]
    data = d // LANES
    pitch = _slab_pitch(d)
    cur = i % 2

    def gather_issue(tok_ref, buf):
        def body(r0, c):
            for u in range(DMA_UNROLL):
                r = r0 * DMA_UNROLL + u
                pltpu.make_async_copy(h_hbm.at[pl.ds(tok_ref[0, 0, r] * pitch, data), :],
                                      xbuf.at[buf, pl.ds(r * pitch, data), :], gsem.at[buf]).start(priority=u % 2)
            return c
        lax.fori_loop(0, MOE_BLOCK // DMA_UNROLL, body, 0)

    def scatter_issue():
        def body(r0, c):
            for u in range(DMA_UNROLL):
                r = r0 * DMA_UNROLL + u
                pltpu.make_async_copy(obuf.at[pl.ds(r * pitch, pitch), :],
                                      y_hbm.at[pl.ds(dst_ref[0, 0, r] * pitch, pitch), :], ssem).start(priority=u % 2)
            return c
        lax.fori_loop(0, MOE_BLOCK // DMA_UNROLL, body, 0)

    def gather_wait(buf):
        n = MOE_BLOCK * data
        pltpu.make_async_copy(h_hbm.at[pl.ds(0, n), :], xbuf.at[buf, pl.ds(0, n), :], gsem.at[buf]).wait()

    def scatter_wait():
        pltpu.make_async_copy(obuf, y_hbm.at[pl.ds(0, MOE_BLOCK * pitch), :], ssem).wait()

    @pl.when(i == 0)
    def _():
        gather_issue(src_ref, 0)
        obuf[...] = jnp.zeros(obuf.shape, F32)
        n_rows = y_hbm.shape[0]
        init = pltpu.make_async_copy(obuf, y_hbm.at[pl.ds(n_rows - MOE_BLOCK * pitch, MOE_BLOCK * pitch), :], ssem)
        init.start()
        init.wait()

    @pl.when(i < n_used)
    def _():
        gather_wait(cur)

        @pl.when(i + 1 < n_used)
        def _():
            gather_issue(src_next_ref, 1 - cur)

        @pl.when(jnp.logical_or(i == 0, be_ref[i] != be_ref[jnp.maximum(i - 1, 0)]))
        def _():
            w1b[...] = w1_ref[...].astype(BF16)
            w3b[...] = w3_ref[...].astype(BF16)
            w2b[...] = w2_ref[...].astype(BF16)

        xb = _slab_load(xbuf.at[cur], MOE_BLOCK, d).astype(BF16)
        a1 = jnp.dot(xb, w1b[...], preferred_element_type=F32)
        a3 = jnp.dot(xb, w3b[...], preferred_element_type=F32)
        u = a1 * (1.0 / (1.0 + jnp.exp(-a1))) * a3
        out = jnp.dot(u.astype(BF16), w2b[...], preferred_element_type=F32)

        @pl.when(i > 0)
        def _():
            scatter_wait()

        _slab_store(obuf, out)
        scatter_issue()

        @pl.when(i == n_used - 1)
        def _():
            scatter_wait()


def _moe(block_expert, n_used, src_tok, dst_row, h2_slab, w1, w3, w2, *, n_out_rows):
    n_blocks = src_tok.shape[0]
    d, ff = w1.shape[1], w1.shape[2]
    slab = _slab_pitch(d)
    grid_spec = pltpu.PrefetchScalarGridSpec(
        num_scalar_prefetch=2,
        grid=(n_blocks,),
        in_specs=[pl.BlockSpec((1, 1, MOE_BLOCK), lambda i, be, nu: (i, 0, 0), memory_space=pltpu.SMEM),
                  pl.BlockSpec((1, 1, MOE_BLOCK), lambda i, be, nu: (jnp.minimum(i + 1, n_blocks - 1), 0, 0),
                               memory_space=pltpu.SMEM),
                  pl.BlockSpec((1, 1, MOE_BLOCK), lambda i, be, nu: (i, 0, 0), memory_space=pltpu.SMEM),
                  pl.BlockSpec(memory_space=pl.ANY),
                  pl.BlockSpec((None, d, ff), lambda i, be, nu: (be[i], 0, 0)),
                  pl.BlockSpec((None, d, ff), lambda i, be, nu: (be[i], 0, 0)),
                  pl.BlockSpec((None, ff, d), lambda i, be, nu: (be[i], 0, 0))],
        out_specs=pl.BlockSpec(memory_space=pl.ANY),
        scratch_shapes=[pltpu.VMEM((2, MOE_BLOCK * slab, LANES), F32), pltpu.VMEM((MOE_BLOCK * slab, LANES), F32),
                        pltpu.VMEM((d, ff), BF16), pltpu.VMEM((d, ff), BF16), pltpu.VMEM((ff, d), BF16),
                        pltpu.SemaphoreType.DMA((2,)), pltpu.SemaphoreType.DMA],
    )
    return pl.pallas_call(
        functools.partial(_moe_kernel, d=d),
        out_shape=jax.ShapeDtypeStruct((n_out_rows * slab, LANES), F32),
        grid_spec=grid_spec,
        compiler_params=_params(("arbitrary",)),
        name="expert_mlp",
    )(block_expert, n_used, src_tok, src_tok, dst_row, h2_slab, w1, w3, w2)


def _routing_plan(route_e, n_tokens):
    n_assign = n_tokens * TOP_K
    e_flat = route_e.reshape(n_assign)
    order = jnp.argsort(e_flat).astype(jnp.int32)
    experts = jnp.arange(N_EXPERTS, dtype=jnp.int32)
    counts = jnp.sum((e_flat[:, None] == experts[None, :]).astype(jnp.int32), axis=0)
    offsets = jnp.cumsum(counts) - counts
    blocks_per_expert = (counts + MOE_BLOCK - 1) // MOE_BLOCK
    block_ends = jnp.cumsum(blocks_per_expert)
    n_blocks = n_assign // MOE_BLOCK + N_EXPERTS
    block_id = jnp.arange(n_blocks, dtype=jnp.int32)
    block_expert = jnp.sum((block_ends[None, :] <= block_id[:, None]).astype(jnp.int32), axis=1)
    block_expert = jnp.minimum(block_expert, N_EXPERTS - 1)
    first_block = (block_ends - blocks_per_expert)[block_expert]
    in_block = jnp.arange(MOE_BLOCK, dtype=jnp.int32)[None, :]
    rank = (block_id - first_block)[:, None] * MOE_BLOCK + in_block
    valid = (rank < counts[block_expert][:, None]) & (block_id < block_ends[-1])[:, None]
    assign = order[jnp.clip(offsets[block_expert][:, None] + rank, 0, n_assign - 1)]
    token = assign // TOP_K
    src_tok = jnp.where(valid, token, 0).astype(jnp.int32)
    dst_row = jnp.where(valid, (assign % TOP_K) * n_tokens + token, n_assign + in_block).astype(jnp.int32)
    shape = (n_blocks, 1, MOE_BLOCK)
    return block_expert, block_ends[-1:].astype(jnp.int32), src_tok.reshape(shape), dst_row.reshape(shape)


def _final_kernel(x1_ref, y0_ref, y1_ref, rw_ref, g_ref, o_ref, *, normalize):
    rw = rw_ref[...]
    lane = lax.broadcasted_iota(jnp.int32, rw.shape, 1)
    w1 = jnp.sum(jnp.where(lane == 0, rw, 0.0), axis=1, keepdims=True)
    w2 = jnp.sum(jnp.where(lane == 1, rw, 0.0), axis=1, keepdims=True)
    rows, d = x1_ref.shape
    x2 = x1_ref[...] + (_slab_load(y0_ref, rows, d) * w1 + _slab_load(y1_ref, rows, d) * w2)
    if normalize:
        ms = jnp.mean(x2 * x2, axis=-1, keepdims=True)
        x2 = x2 * lax.rsqrt(ms + EPS) * g_ref[...]
    o_ref[...] = x2


def _final(x1, y2, route_w, g, *, normalize):
    t, d = x1.shape
    tm = MERGE_TILE
    nt = t // tm
    return pl.pallas_call(
        functools.partial(_final_kernel, normalize=normalize),
        out_shape=jax.ShapeDtypeStruct((t, d), F32),
        grid=(nt,),
        in_specs=[pl.BlockSpec((tm, d), lambda i: (i, 0)),
                  pl.BlockSpec((tm * _slab_pitch(d), LANES), lambda i: (i, 0)),
                  pl.BlockSpec((tm * _slab_pitch(d), LANES), lambda i: (i + nt, 0)),
                  pl.BlockSpec((tm, ROUTER_LANES), lambda i: (i, 0)),
                  pl.BlockSpec((1, d), lambda i: (0, 0))],
        out_specs=pl.BlockSpec((tm, d), lambda i: (i, 0)),
        compiler_params=_params(("parallel",)),
        name="combine_norm",
    )(x1, y2, y2, route_w, g.reshape(1, d))


def _rope_tables(seq):
    pos = jnp.arange(seq, dtype=jnp.int32)
    inv = ROPE_THETA ** (-jnp.arange(0, 2 * ROPE_HALF, 2, dtype=F32) / (2 * ROPE_HALF))

    def cs(p):
        ang = p.astype(F32)[:, None] * inv[None, :]
        return jnp.cos(ang), jnp.sin(ang)

    cr, sr = cs(pos // GRID_W)
    cc, sc = cs(pos % GRID_W)
    cp, sp = cs(pos)
    axial = (jnp.concatenate([cr, cr, cc, cc], axis=1), jnp.concatenate([-sr, sr, -sc, sc], axis=1))
    linear = (jnp.concatenate([cp, cp, cp, cp], axis=1), jnp.concatenate([-sp, sp, -sp, sp], axis=1))
    return axial, linear


def kernel(x, g_mix, w_in, q_norm_a, k_norm_a, lam_q1, lam_k1, lam_q2, lam_k2, subln_b, w_branch_a,
           w_branch_b, w_gate, b_gate, w_out, g_ffn, w_router_group, w_router_expert, w_e_gate, w_e_up,
           w_e_down, g_final):
    batch, seq, d = x.shape
    t = batch * seq
    depth = g_mix.shape[0]
    a_q = A_HEADS * HEAD
    a_kv = A_KV_HEADS * HEAD
    b_w = B_HEADS * HEAD
    cuts = [0, a_q, a_q + a_kv, a_q + 2 * a_kv, a_q + 2 * a_kv + b_w, a_q + 2 * a_kv + 2 * b_w,
            a_q + 2 * a_kv + 3 * b_w]
    (cos_a, sin_a), (cos_b, sin_b) = _rope_tables(seq)
    scale_a = HEAD ** -0.5 * math.log2(math.e)
    scale_b = B_QK_DIM ** -0.5 * math.log2(math.e)
    ones = jnp.ones((HEAD,), F32)

    xf = x.reshape(t, d)
    for l in range(depth):
        lam_init = 0.8 - 0.6 * math.exp(-0.3 * l)
        w = w_in[l].astype(BF16)
        w_qa, w_ka, w_va, w_qb, w_kb, w_vb = (w[:, cuts[i]:cuts[i + 1]] for i in range(6))

        h = _rmsnorm(xf, g_mix[l], BF16)
        qa =_proj_rope(h, w_qa, cos_a * scale_a, sin_a * scale_a, q_norm_a[l], normalize=True, seq=seq)
        ka = _proj_rope(h, w_ka, cos_a, sin_a, k_norm_a[l], normalize=True, seq=seq)
        vat = _proj_t(h, w_va.T)
        qb = _proj_rope(h, w_qb, cos_b * scale_b, sin_b * scale_b, ones, normalize=False, seq=seq)
        kb = _proj_rope(h, w_kb, cos_b, sin_b, ones, normalize=False, seq=seq)
        vbt = _proj_t(h, w_vb.T)
        gates = _proj_gate(h, w_gate[l].astype(BF16), b_gate[l])

        o_a = _attention(qa, ka, vat, [], batch=batch, seq=seq, kv_heads=A_KV_HEADS, q_cols=A_GROUP * HEAD,
                         differential=False, lam_init=lam_init)
        lam_rows = [v[l].reshape(1, B_QK_DIM) for v in (lam_q1, lam_k1, lam_q2, lam_k2)]
        o_b = _attention(qb, kb, vbt, lam_rows + [subln_b[l].reshape(HEAD, 1)], batch=batch, seq=seq,
                         kv_heads=B_HEADS, q_cols=HEAD, differential=True, lam_init=lam_init)

        w_r = jnp.zeros((d, ROUTER_LANES), F32)
        w_r = w_r.at[:, :N_GROUPS].set(w_router_group[l]).at[:, N_GROUPS:N_GROUPS + N_EXPERTS].set(w_router_expert[l])
        x1, h2, route_e, route_w = _merge(o_a, o_b, gates, xf, w_branch_a[l].astype(BF16),
                                          w_branch_b[l].astype(BF16), w_out[l].astype(BF16), g_ffn[l],
                                          w_r.astype(BF16))

        block_expert, n_used, src_tok, dst_row = _routing_plan(route_e[:, :TOP_K], t)
        y2 = _moe(block_expert, n_used, src_tok, dst_row, h2, w_e_gate[l], w_e_up[l], w_e_down[l],
                  n_out_rows=TOP_K * t + MOE_BLOCK)
        xf = _final(x1, y2, route_w, g_final, normalize=(l + 1 == depth))
    return xf.reshape(batch, seq, d)
```
